```python
import math
import jax, jax.numpy as jnp
from jax import lax
import numpy as np

D_MODEL = 1024
BATCH = 2
SEQ = 8192
DEPTH = 4
DEC_BATCH = 32
DEC_SEQ = 8
PAST_LEN = 8192
PAGE_SIZE = 128

N_MIXERS = 2
N_ATTN_LAYERS = (DEPTH + 1) // 2
N_SSD_LAYERS = DEPTH // 2

ATT_HEADS = 8
ATT_HD = D_MODEL // ATT_HEADS // 2
ATT_VD = 2 * ATT_HD
Q_BLOCK = 128

REL_BUCKETS = 32
REL_MAX_DIST = 128

SSD_EXPAND = 2
SSD_D_INNER = SSD_EXPAND * D_MODEL
SSD_HEADDIM = 64
SSD_HEADS = SSD_D_INNER // SSD_HEADDIM
SSD_GROUPS = 4
SSD_HPG = SSD_HEADS // SSD_GROUPS
SSD_STATE = 128
SSD_CONV = 4
SSD_CONV_DIM = SSD_D_INNER + 2 * SSD_GROUPS * SSD_STATE
SSD_CHUNK = 128

FFN_HIDDEN = -(-8 * D_MODEL // (3 * 256)) * 256

DEEPNORM_ALPHA = (2 * DEPTH) ** 0.25
DEEPNORM_BETA = (8 * DEPTH) ** -0.25
LN_EPS = 1e-5
RMS_EPS = 1e-5
F32 = jnp.float32

kernel_name = 'diff_attn_mamba2_hybrid_step'


def layer_norm(x, g, b):
    xf = x.astype(F32)
    mu = jnp.mean(xf, axis=-1, keepdims=True)
    var = jnp.mean(jnp.square(xf - mu), axis=-1, keepdims=True)
    return ((xf - mu) * lax.rsqrt(var + LN_EPS) * g.astype(F32) + b.astype(F32)).astype(x.dtype)


def rms_normalize(x):
    xf = x.astype(F32)
    return xf * lax.rsqrt(jnp.mean(jnp.square(xf), axis=-1, keepdims=True) + RMS_EPS)


def modulate(x, c, w, b):
    m = jax.nn.silu(c) @ w + b
    shift, scale, gate = jnp.split(m[:, None, :], 3, axis=-1)
    return x * (1 + scale) + shift, gate


def post_norm(x, gate, out, g, b):
    return layer_norm(DEEPNORM_ALPHA * x + gate * out.astype(x.dtype), g, b)


def rel_bucket(q_pos, k_pos):
    n = jnp.maximum(q_pos[:, None] - k_pos[None, :], 0)
    max_exact = REL_BUCKETS // 2
    nf = jnp.maximum(n, 1).astype(F32)
    large = max_exact + (jnp.log(nf / max_exact) / math.log(REL_MAX_DIST / max_exact)
                         * (REL_BUCKETS - max_exact)).astype(jnp.int32)
    return jnp.where(n < max_exact, n, jnp.minimum(large, REL_BUCKETS - 1))


def diff_logits(q, k, q_pos, k_pos, rel_bias):
    s = jnp.einsum('bqhmd,bkhmd->bmhqk', q, k).astype(F32)
    bias = rel_bias.astype(F32).T[:, rel_bucket(q_pos, k_pos)]
    mask = k_pos[None, :] <= q_pos[:, None]
    return jnp.where(mask, s + bias, -jnp.inf)


def diff_weights(logits, lam):
    p = jax.nn.softmax(logits, axis=-1)
    return p[:, 0] - lam * p[:, 1]


def diff_lambda(lam_p, lam_init):
    lp = lam_p.astype(F32)
    return jnp.exp(jnp.sum(lp[0] * lp[1])) - jnp.exp(jnp.sum(lp[2] * lp[3])) + lam_init


def diff_qkv(h, w_qkv):
    bsz, t, _ = h.shape
    q, k, v = jnp.split(h @ w_qkv, 3, axis=-1)
    q = q.reshape(bsz, t, ATT_HEADS, 2, ATT_HD) * (ATT_HD ** -0.5)
    k = k.reshape(bsz, t, ATT_HEADS, 2 * ATT_HD)
    v = v.reshape(bsz, t, ATT_HEADS, ATT_VD)
    return q, k, v


def diff_out(o, subln_g, lam_init, w_o, dtype):
    bsz, t = o.shape[:2]
    o = rms_normalize(o) * subln_g.astype(F32) * (1.0 - lam_init)
    return o.reshape(bsz, t, ATT_HEADS * ATT_VD).astype(dtype) @ w_o


def diff_attn_prompt(h, w_qkv, lam_p, subln_g, w_o, rel_bias, lam_init):
    bsz, t, _ = h.shape
    q, k, v = diff_qkv(h, w_qkv)
    lam = diff_lambda(lam_p, lam_init)
    pos = jnp.arange(t)
    k_split = k.reshape(bsz, t, ATT_HEADS, 2, ATT_HD)
    v32 = v.astype(F32)
    nb = t // Q_BLOCK
    q_blocks = jnp.moveaxis(q.reshape(bsz, nb, Q_BLOCK, ATT_HEADS, 2, ATT_HD), 1, 0)
    pos_blocks = pos.reshape(nb, Q_BLOCK)

    def one_block(args):
        q_blk, p_blk = args
        w = diff_weights(diff_logits(q_blk, k_split, p_blk, pos, rel_bias), lam)
        return jnp.einsum('bhqk,bkhd->bqhd', w, v32)

    o = lax.map(one_block, (q_blocks, pos_blocks))
    o = jnp.moveaxis(o, 0, 1).reshape(bsz, t, ATT_HEADS, ATT_VD)
    return diff_out(o, subln_g, lam_init, w_o, h.dtype), k, v


def diff_attn_sample(h, k_pool, v_pool, page_table, w_qkv, lam_p, subln_g, w_o, rel_bias, lam_init):
    bsz, t, _ = h.shape
    q, k, v = diff_qkv(h, w_qkv)
    lam = diff_lambda(lam_p, lam_init)
    past = page_table.shape[1] * PAGE_SIZE
    k_past = k_pool[page_table].reshape(bsz, past, ATT_HEADS, 2, ATT_HD)
    v_past = v_pool[page_table].reshape(bsz, past, ATT_HEADS, ATT_VD)
    past_pos = jnp.arange(past)
    q_pos = past + jnp.arange(t)
    logits = jnp.concatenate([
        diff_logits(q, k_past, q_pos, past_pos, rel_bias),
        diff_logits(q, k.reshape(bsz, t, ATT_HEADS, 2, ATT_HD), q_pos, q_pos, rel_bias)], axis=-1)
    w = diff_weights(logits, lam)
    o = (jnp.einsum('bhqk,bkhd->bqhd', w[..., :past], v_past.astype(F32))
         + jnp.einsum('bhqk,bkhd->bqhd', w[..., past:], v.astype(F32)))
    return diff_out(o, subln_g, lam_init, w_o, h.dtype), k, v


def causal_conv(xpad, w, b):
    y = lax.conv_general_dilated(xpad, w[:, None, :].astype(xpad.dtype), window_strides=(1,),
                                 padding='VALID', dimension_numbers=('NWC', 'WIO', 'NWC'),
                                 feature_group_count=xpad.shape[-1])
    return y + b


def ssd_scan(x, dt, a, bm, cm, init):
    bsz, t = x.shape[:2]
    chunk = SSD_CHUNK if t % SSD_CHUNK == 0 else t
    nc = t // chunk
    xd = (x.astype(F32) * dt[..., None]).reshape(bsz, nc, chunk, SSD_GROUPS, SSD_HPG, SSD_HEADDIM)
    adt = (a * dt).reshape(bsz, nc, chunk, SSD_GROUPS, SSD_HPG)
    bm = bm.astype(F32).reshape(bsz, nc, chunk, SSD_GROUPS, SSD_STATE)
    cm = cm.astype(F32).reshape(bsz, nc, chunk, SSD_GROUPS, SSD_STATE)
    acs = jnp.cumsum(adt, axis=2)
    causal = jnp.tril(jnp.ones((chunk, chunk), dtype=bool))[:, :, None, None]
    seg = acs[:, :, :, None] - acs[:, :, None, :]
    decay_ls = jnp.exp(jnp.where(causal, seg, -jnp.inf))
    cb = jnp.einsum('bclgn,bcsgn->bclsg', cm, bm)
    y_diag = jnp.einsum('bclsgr,bcsgrp->bclgrp', cb[..., None] * decay_ls, xd)
    decay_to_end = jnp.exp(acs[:, :, -1:] - acs)
    chunk_states = jnp.einsum('bclgn,bclgr,bclgrp->bcgrpn', bm, decay_to_end, xd)
    chunk_decay = jnp.exp(acs[:, :, -1])

    def step(s, inp):
        s_c, d_c = inp
        return s * d_c[..., None, None] + s_c, s

    final, prev = lax.scan(step, init, (jnp.moveaxis(chunk_states, 1, 0), jnp.moveaxis(chunk_decay, 1, 0)))
    prev = jnp.moveaxis(prev, 0, 1)
    y_off = jnp.einsum('bclgn,bcgrpn,bclgr->bclgrp', cm, prev, jnp.exp(acs))
    return (y_diag + y_off).reshape(bsz, t, SSD_GROUPS, SSD_HPG, SSD_HEADDIM), final


def ssd_mixer(h, conv_prefix, ssm_init, w_in, conv_w, conv_b, dt_bias, a_log, d_skip, norm_g, w_out):
    bsz, t, _ = h.shape
    gn = SSD_GROUPS * SSD_STATE
    zxbcdt = h @ w_in
    z = zxbcdt[..., :SSD_D_INNER]
    xbc = zxbcdt[..., SSD_D_INNER:SSD_D_INNER + SSD_CONV_DIM]
    dt = zxbcdt[..., SSD_D_INNER + SSD_CONV_DIM:]
    xpad = jnp.concatenate([conv_prefix.astype(xbc.dtype), xbc], axis=1)
    new_conv = xpad[:, -(SSD_CONV - 1):]
    xbc = jax.nn.silu(causal_conv(xpad, conv_w, conv_b))
    x = xbc[..., :SSD_D_INNER].reshape(bsz, t, SSD_GROUPS, SSD_HPG, SSD_HEADDIM)
    bm = xbc[..., SSD_D_INNER:SSD_D_INNER + gn].reshape(bsz, t, SSD_GROUPS, SSD_STATE)
    cm = xbc[..., SSD_D_INNER + gn:].reshape(bsz, t, SSD_GROUPS, SSD_STATE)
    dt = jax.nn.softplus(dt.astype(F32) + dt_bias.astype(F32)).reshape(bsz, t, SSD_GROUPS, SSD_HPG)
    a = -jnp.exp(a_log.astype(F32)).reshape(SSD_GROUPS, SSD_HPG)
    init = ssm_init.astype(F32).reshape(bsz, SSD_GROUPS, SSD_HPG, SSD_HEADDIM, SSD_STATE)
    y, final = ssd_scan(x, dt, a, bm, cm, init)
    y = y + x.astype(F32) * d_skip.astype(F32).reshape(SSD_GROUPS, SSD_HPG)[..., None]
    y = y.reshape(bsz, t, SSD_D_INNER) * jax.nn.silu(z.astype(F32))
    y = rms_normalize(y.reshape(bsz, t, SSD_GROUPS, SSD_D_INNER // SSD_GROUPS)).reshape(bsz, t, SSD_D_INNER)
    out = (y * norm_g.astype(F32)).astype(h.dtype) @ w_out
    final = final.reshape(bsz, SSD_HEADS, SSD_HEADDIM, SSD_STATE).astype(h.dtype)
    return out, final, new_conv


def swiglu(h, w_in, w_out):
    g, u = jnp.split(h @ w_in, 2, axis=-1)
    return (jax.nn.silu(g) * u) @ w_out


def setup_inputs(seed: int = 0) -> dict:
    key = jax.random.key(seed)
    keys = jax.random.split(key, 40)
    it = iter(range(40))

    def nrm(shape, scale):
        return scale * jax.random.normal(keys[next(it)], shape, F32)

    d = D_MODEL
    n_pages = PAST_LEN // PAGE_SIZE
    n_used = DEC_BATCH * n_pages
    n_pool = n_used + max(1, n_used // 4)
    page_table = jax.random.permutation(keys[next(it)], n_pool)[:n_used].reshape(DEC_BATCH, n_pages).astype(jnp.int32)
    ssd_in_width = 2 * SSD_D_INNER + 2 * SSD_GROUPS * SSD_STATE + SSD_HEADS
    dt_init = jnp.exp(jax.random.uniform(keys[next(it)], (N_SSD_LAYERS, SSD_HEADS), F32,
                                         math.log(1e-3), math.log(1e-1)))
    dt_bias = dt_init + jnp.log(-jnp.expm1(-dt_init))
    a_log = jnp.log(jax.random.uniform(keys[next(it)], (N_SSD_LAYERS, SSD_HEADS), F32, 1.0, 16.0))
    return {
        'x_prompt': nrm((BATCH, SEQ, d), 1.0),
        'x_sample': nrm((DEC_BATCH, DEC_SEQ, d), 1.0),
        'cache_k': nrm((N_ATTN_LAYERS, n_pool, PAGE_SIZE, ATT_HEADS, 2 * ATT_HD), 1.0),
        'cache_v': nrm((N_ATTN_LAYERS, n_pool, PAGE_SIZE, ATT_HEADS, ATT_VD), 1.0),
        'state_ssm': nrm((N_SSD_LAYERS, DEC_BATCH, SSD_HEADS, SSD_HEADDIM, SSD_STATE), 0.5),
        'state_conv': nrm((N_SSD_LAYERS, DEC_BATCH, SSD_CONV - 1, SSD_CONV_DIM), 1.0),
        'page_table': page_table,
        'c_prompt': nrm((BATCH, d), 1.0),
        'c_sample': nrm((DEC_BATCH, d), 1.0),
        'ada_w': nrm((DEPTH, 2, d, 3 * d), d ** -0.5),
        'ada_b': nrm((DEPTH, 2, 3 * d), 0.02),
        'ln_g': 1.0 + nrm((DEPTH, 2, d), 0.02),
        'ln_b': nrm((DEPTH, 2, d), 0.02),
        'rel_bias': nrm((REL_BUCKETS, ATT_HEADS), 0.5),
        'attn_w_qkv': nrm((N_ATTN_LAYERS, d, 3 * ATT_HEADS * 2 * ATT_HD), d ** -0.5),
        'attn_lambda': nrm((N_ATTN_LAYERS, 4, ATT_HD), 0.1),
        'attn_subln_g': 1.0 + nrm((N_ATTN_LAYERS, ATT_VD), 0.02),
        'attn_w_o': nrm((N_ATTN_LAYERS, ATT_HEADS * ATT_VD, d), (ATT_HEADS * ATT_VD) ** -0.5 * DEEPNORM_BETA),
        'ssd_w_in': nrm((N_SSD_LAYERS, d, ssd_in_width), d ** -0.5),
        'ssd_conv_w': nrm((N_SSD_LAYERS, SSD_CONV, SSD_CONV_DIM), SSD_CONV ** -0.5),
        'ssd_conv_b': nrm((N_SSD_LAYERS, SSD_CONV_DIM), 0.02),
        'ssd_dt_bias': dt_bias,
        'ssd_a_log': a_log,
        'ssd_d': 1.0 + nrm((N_SSD_LAYERS, SSD_HEADS), 0.1),
        'ssd_norm_g': 1.0 + nrm((N_SSD_LAYERS, SSD_D_INNER), 0.02),
        'ssd_w_out': nrm((N_SSD_LAYERS, SSD_D_INNER, d), SSD_D_INNER ** -0.5 * DEEPNORM_BETA),
        'ffn_w_in': nrm((DEPTH, d, 2 * FFN_HIDDEN), d ** -0.5),
        'ffn_w_out': nrm((DEPTH, FFN_HIDDEN, d), FFN_HIDDEN ** -0.5 * DEEPNORM_BETA),
    }


def reference(x_prompt, x_sample, cache_k, cache_v, state_ssm, state_conv, page_table, c_prompt, c_sample,
              ada_w, ada_b, ln_g, ln_b, rel_bias, attn_w_qkv, attn_lambda, attn_subln_g, attn_w_o,
              ssd_w_in, ssd_conv_w, ssd_conv_b, ssd_dt_bias, ssd_a_log, ssd_d, ssd_norm_g, ssd_w_out,
              ffn_w_in, ffn_w_out):
    y_p, y_s = x_prompt, x_sample
    k_p, v_p, k_s, v_s = [], [], [], []
    ssm_p, conv_p, ssm_s, conv_s = [], [], [], []
    for i in range(DEPTH):
        j = i // N_MIXERS
        h_p, g_p = modulate(y_p, c_prompt, ada_w[i, 0], ada_b[i, 0])
        h_s, g_s = modulate(y_s, c_sample, ada_w[i, 0], ada_b[i, 0])
        if i % N_MIXERS == 0:
            lam_init = 0.8 - 0.6 * math.exp(-0.3 * i)
            attn_args = (attn_w_qkv[j], attn_lambda[j], attn_subln_g[j], attn_w_o[j], rel_bias, lam_init)
            o_p, kk, vv = diff_attn_prompt(h_p, *attn_args)
            k_p.append(kk)
            v_p.append(vv)
            o_s, kk, vv = diff_attn_sample(h_s, cache_k[j], cache_v[j], page_table, *attn_args)
            k_s.append(kk)
            v_s.append(vv)
        else:
            ssd_args = (ssd_w_in[j], ssd_conv_w[j], ssd_conv_b[j], ssd_dt_bias[j], ssd_a_log[j],
                        ssd_d[j], ssd_norm_g[j], ssd_w_out[j])
            zero_conv = jnp.zeros((h_p.shape[0], SSD_CONV - 1, SSD_CONV_DIM), h_p.dtype)
            zero_ssm = jnp.zeros((h_p.shape[0], SSD_HEADS, SSD_HEADDIM, SSD_STATE), F32)
            o_p, st, cv = ssd_mixer(h_p, zero_conv, zero_ssm, *ssd_args)
            ssm_p.append(st)
            conv_p.append(cv)
            o_s, st, cv = ssd_mixer(h_s, state_conv[j], state_ssm[j], *ssd_args)
            ssm_s.append(st)
            conv_s.append(cv)
        y_p = post_norm(y_p, g_p, o_p, ln_g[i, 0], ln_b[i, 0])
        y_s = post_norm(y_s, g_s, o_s, ln_g[i, 0], ln_b[i, 0])
        h_p, g_p = modulate(y_p, c_prompt, ada_w[i, 1], ada_b[i, 1])
        h_s, g_s = modulate(y_s, c_sample, ada_w[i, 1], ada_b[i, 1])
        y_p = post_norm(y_p, g_p, swiglu(h_p, ffn_w_in[i], ffn_w_out[i]), ln_g[i, 1], ln_b[i, 1])
        y_s = post_norm(y_s, g_s, swiglu(h_s, ffn_w_in[i], ffn_w_out[i]), ln_g[i, 1], ln_b[i, 1])
    return (y_p, y_s, jnp.stack(k_p), jnp.stack(v_p), jnp.stack(k_s), jnp.stack(v_s),
            jnp.stack(ssm_p), jnp.stack(conv_p), jnp.stack(ssm_s), jnp.stack(conv_s))
```

```python
import functools
import math

import jax
import jax.numpy as jnp
from jax import lax
from jax.experimental import pallas as pl
from jax.experimental.pallas import tpu as pltpu

F32 = jnp.float32
BF16 = jnp.bfloat16

D_MODEL = 1024
DEPTH = 4
PAGE_SIZE = 128
ATT_HEADS = 8
ATT_HD = 64
ATT_VD = 128
REL_BUCKETS = 32
REL_MAX_DIST = 128
SSD_D_INNER = 2048
SSD_HEADDIM = 64
SSD_HEADS = 32
SSD_GROUPS = 4
SSD_STATE = 128
SSD_CONV = 4
SSD_GN = SSD_GROUPS * SSD_STATE
SSD_CONV_DIM = SSD_D_INNER + 2 * SSD_GN
SSD_CHUNK = 128
FFN_HIDDEN = 2816
DEEPNORM_ALPHA = (2 * DEPTH) ** 0.25
LN_EPS = 1e-5
RMS_EPS = 1e-5

LANES = 128
DT_PAD = LANES
NEG_BIG = -1e30
VMEM_LIMIT = 56 * 1024 * 1024
ATTN_TILE = 512
PAGES_PER_STEP = 8
FFN_CHUNK = 256


def _cparams(*sem):
    return pltpu.CompilerParams(dimension_semantics=sem, vmem_limit_bytes=VMEM_LIMIT)


def _silu(x):
    return x * jax.nn.sigmoid(x)


def _layer_norm(r, g, b):
    mu = jnp.mean(r, axis=-1, keepdims=True)
    d = r - mu
    var = jnp.mean(d * d, axis=-1, keepdims=True)
    return d * lax.rsqrt(var + LN_EPS) * g + b


def _mod_specs(mod_rows, tm):
    rows = 1 if mod_rows == 1 else tm

    def spec(piece):
        if mod_rows == 1:
            return pl.BlockSpec((None, 1, D_MODEL), lambda b, t: (b, 0, piece))
        return pl.BlockSpec((None, rows, D_MODEL), lambda b, t: (b, t, piece))

    return spec(0), spec(1), spec(2)


def _const_spec(shape):
    n = len(shape)
    return pl.BlockSpec(shape, lambda b, t: (0,) * n)


def _mod_kernel(c_ref, w_ref, b_ref, o_ref):
    a = _silu(c_ref[...]).astype(BF16)
    o_ref[...] = jnp.dot(a, w_ref[...].astype(BF16), preferred_element_type=F32) + b_ref[...]


def _mod_call(c_all, ada_w, ada_b):
    n_sub, d, d3 = ada_w.shape
    rc = c_all.shape[0]
    tn = 1024
    return pl.pallas_call(
        _mod_kernel,
        grid=(n_sub, d3 // tn),
        in_specs=[
            pl.BlockSpec((rc, d), lambda i, j: (0, 0)),
            pl.BlockSpec((None, d, tn), lambda i, j: (i, 0, j)),
            pl.BlockSpec((None, 1, tn), lambda i, j: (i, 0, j)),
        ],
        out_specs=pl.BlockSpec((None, rc, tn), lambda i, j: (i, 0, j)),
        out_shape=jax.ShapeDtypeStruct((n_sub, rc, d3), F32),
        compiler_params=_cparams("arbitrary", "arbitrary"),
        name="adaln_mod",
    )(c_all, ada_w, ada_b.reshape(n_sub, 1, d3))


def _qkv_kernel(x_ref, sh_ref, sc_ref, w_ref, q_ref, k_ref, v_ref, kb_ref, vb_ref):
    d = D_MODEL
    h = (x_ref[...] * (1.0 + sc_ref[...]) + sh_ref[...]).astype(BF16)
    q = jnp.dot(h, w_ref[:, 0:d], preferred_element_type=F32)
    q_ref[...] = (q * (ATT_HD ** -0.5)).astype(BF16)
    tm = x_ref.shape[0]
    k = jnp.dot(h, w_ref[:, d:2 * d], preferred_element_type=F32)
    kb_ref[...] = k.astype(BF16)
    v = jnp.dot(h, w_ref[:, 2 * d:3 * d], preferred_element_type=F32)
    vb_ref[...] = v.astype(BF16)
    for hd in range(ATT_HEADS):
        hs = slice(hd * LANES, (hd + 1) * LANES)
        k_ref[pl.ds(hd, tm, stride=ATT_HEADS), :] = k[:, hs]
        v_ref[pl.ds(hd, tm, stride=ATT_HEADS), :] = v[:, hs]


def _qkv_call(x, mod, w_bf):
    bx, tx, d = x.shape
    tm = min(512, tx)
    sh, sc, _ = _mod_specs(mod.shape[1], tm)
    tok = pl.BlockSpec((None, tm, d), lambda b, t: (b, t, 0))
    rows = pl.BlockSpec((None, tm * ATT_HEADS, LANES), lambda b, t: (b, t, 0))
    return pl.pallas_call(
        _qkv_kernel,
        grid=(bx, tx // tm),
        in_specs=[tok, sh, sc, _const_spec(w_bf.shape)],
        out_specs=[tok, rows, rows, tok, tok],
        out_shape=[
            jax.ShapeDtypeStruct((bx, tx, d), BF16),
            jax.ShapeDtypeStruct((bx, tx * ATT_HEADS, LANES), F32),
            jax.ShapeDtypeStruct((bx, tx * ATT_HEADS, LANES), F32),
            jax.ShapeDtypeStruct((bx, tx, d), BF16),
            jax.ShapeDtypeStruct((bx, tx, d), BF16),
        ],
        compiler_params=_cparams("arbitrary", "arbitrary"),
        name="qkv_proj",
    )(x, mod, mod, w_bf)


def _stack_maps(q):
    lane = lax.broadcasted_iota(jnp.int32, q.shape, 1)
    zero = jnp.zeros_like(q)
    return jnp.concatenate([jnp.where(lane < ATT_HD, q, zero), jnp.where(lane >= ATT_HD, q, zero)], axis=0)


def _lambda_full(lam_ref, lam_init):
    lp = lam_ref[...]
    a = jnp.sum(lp[0:1, :] * lp[1:2, :], axis=-1, keepdims=True)
    b = jnp.sum(lp[2:3, :] * lp[3:4, :], axis=-1, keepdims=True)
    return jnp.exp(a) - jnp.exp(b) + lam_init


def _softmax_step(s, v, m_prev, l_prev, acc_prev):
    m_new = jnp.maximum(m_prev, jnp.max(s, axis=-1, keepdims=True))
    alpha = jnp.exp(m_prev - m_new)
    p = jnp.exp(s - m_new)
    l_new = alpha * l_prev + jnp.sum(p, axis=-1, keepdims=True)
    acc_new = alpha * acc_prev + jnp.dot(p.astype(BF16), v, preferred_element_type=F32)
    return m_new, l_new, acc_new


def _diff_finish(acc, l, t, lam, g, lam_init):
    o = acc[0:t] / l[0:t] - lam * (acc[t:2 * t] / l[t:2 * t])
    ms = jnp.mean(o * o, axis=-1, keepdims=True)
    return o * lax.rsqrt(ms + RMS_EPS) * g * (1.0 - lam_init)


def _nt_dot(a, b):
    return lax.dot_general(a, b, (((1,), (1,)), ((), ())), preferred_element_type=F32)


def _attn_prompt_kernel(qi_tab, ki_tab, q_ref, k_ref, v_ref, bias_ref, cfar_ref, lam_ref, g_ref,
                        o_ref, m_sc, l_sc, acc_sc, *, t, lam_init):
    h = pl.program_id(1)
    step = pl.program_id(2)
    qi = qi_tab[step]
    ki = ki_tab[step]

    @pl.when(ki == 0)
    def _():
        m_sc[...] = jnp.full(m_sc.shape, NEG_BIG, F32)
        l_sc[...] = jnp.zeros(l_sc.shape, F32)
        acc_sc[...] = jnp.zeros(acc_sc.shape, F32)

    def update(bias):
        s = _nt_dot(_stack_maps(q_ref[...]), k_ref[...])
        if bias is None:
            s = s + cfar_ref[h]
        else:
            s = s + jnp.concatenate([bias, bias], axis=0)
        m, l, acc = _softmax_step(s, v_ref[...], m_sc[...], l_sc[...], acc_sc[...])
        m_sc[...] = m
        l_sc[...] = l
        acc_sc[...] = acc

    @pl.when(ki < qi - 1)
    def _():
        update(None)

    @pl.when(ki == qi - 1)
    def _():
        update(bias_ref[1])

    @pl.when(ki == qi)
    def _():
        update(bias_ref[0])
        lam = _lambda_full(lam_ref, lam_init)
        o_ref[...] = _diff_finish(acc_sc[...], l_sc[...], t, lam, g_ref[...], lam_init).astype(BF16)


def _attn_prompt_call(q_bf, k_bf, v_bf, bias_tiles, cfar, lam_p, g, lam_init):
    b, tt, d = q_bf.shape
    t = bias_tiles.shape[-1]
    nq = tt // t
    qi_list, ki_list = [], []
    for qi in range(nq):
        for ki in range(qi + 1):
            qi_list.append(qi)
            ki_list.append(ki)
    qi_tab = jnp.asarray(qi_list, jnp.int32)
    ki_tab = jnp.asarray(ki_list, jnp.int32)
    grid_spec = pltpu.PrefetchScalarGridSpec(
        num_scalar_prefetch=2,
        grid=(b, ATT_HEADS, len(qi_list)),
        in_specs=[
            pl.BlockSpec((None, t, LANES), lambda b, h, s, qt, kt: (b, qt[s], h)),
            pl.BlockSpec((None, t, LANES), lambda b, h, s, qt, kt: (b, kt[s], h)),
            pl.BlockSpec((None, t, LANES), lambda b, h, s, qt, kt: (b, kt[s], h)),
            pl.BlockSpec((None, 2, t, t), lambda b, h, s, qt, kt: (h, 0, 0, 0)),
            pl.BlockSpec(memory_space=pltpu.SMEM),
            pl.BlockSpec((4, ATT_HD), lambda b, h, s, qt, kt: (0, 0)),
            pl.BlockSpec((1, ATT_VD), lambda b, h, s, qt, kt: (0, 0)),
        ],
        out_specs=pl.BlockSpec((None, t, LANES), lambda b, h, s, qt, kt: (b, qt[s], h)),
        scratch_shapes=[
            pltpu.VMEM((2 * t, 1), F32),
            pltpu.VMEM((2 * t, 1), F32),
            pltpu.VMEM((2 * t, ATT_VD), F32),
        ],
    )
    return pl.pallas_call(
        functools.partial(_attn_prompt_kernel, t=t, lam_init=lam_init),
        grid_spec=grid_spec,
        out_shape=jax.ShapeDtypeStruct((b, tt, d), BF16),
        compiler_params=_cparams("arbitrary", "arbitrary", "arbitrary"),
        name="attn_prompt",
    )(qi_tab, ki_tab, q_bf, k_bf, v_bf, bias_tiles, cfar, lam_p, g.reshape(1, ATT_VD))


def _attn_sample_kernel(pt_ref, q_ref, kn_ref, vn_ref, *rest, tq, n_pages, lam_init):
    k_pages = rest[:n_pages]
    v_pages = rest[n_pages:2 * n_pages]
    bias_last_ref, bias_new_ref, cfar_ref, lam_ref, g_ref, o_ref, k_sc, v_sc, m_sc, l_sc, acc_sc = rest[2 * n_pages:]
    c = pl.program_id(1)
    last = pl.num_programs(1) - 1

    @pl.when(c == 0)
    def _():
        m_sc[...] = jnp.full(m_sc.shape, NEG_BIG, F32)
        l_sc[...] = jnp.zeros(l_sc.shape, F32)
        acc_sc[...] = jnp.zeros(acc_sc.shape, F32)

    for j in range(n_pages):
        rs = slice(j * PAGE_SIZE, (j + 1) * PAGE_SIZE)
        for h in range(ATT_HEADS):
            k_sc[h, rs, :] = k_pages[j][pl.ds(h, PAGE_SIZE, stride=ATT_HEADS), :].astype(BF16)
            v_sc[h, rs, :] = v_pages[j][pl.ds(h, PAGE_SIZE, stride=ATT_HEADS), :].astype(BF16)

    q = q_ref[...].astype(F32)

    def head_q(h):
        return _stack_maps(q[:, h * LANES:(h + 1) * LANES]).astype(BF16)

    def update(h, s, v):
        m, l, acc = _softmax_step(s, v, m_sc[h], l_sc[h], acc_sc[h])
        m_sc[h] = m
        l_sc[h] = l
        acc_sc[h] = acc

    def past(h, bias):
        s = _nt_dot(head_q(h), k_sc[h])
        update(h, s + bias, v_sc[h])

    @pl.when(c < last)
    def _():
        for h in range(ATT_HEADS):
            past(h, cfar_ref[h])

    @pl.when(c == last)
    def _():
        lam = _lambda_full(lam_ref, lam_init)
        pad = jnp.zeros((LANES - tq, D_MODEL), F32)
        kn = jnp.concatenate([kn_ref[...].astype(F32), pad], axis=0).astype(BF16)
        vn = jnp.concatenate([vn_ref[...].astype(F32), pad], axis=0).astype(BF16)
        for h in range(ATT_HEADS):
            hs = slice(h * LANES, (h + 1) * LANES)
            past(h, bias_last_ref[h])
            s = _nt_dot(head_q(h), kn[:, hs]) + bias_new_ref[h]
            update(h, s, vn[:, hs])
            o_ref[:, hs] = _diff_finish(acc_sc[h], l_sc[h], tq, lam, g_ref[...], lam_init).astype(BF16)


def _attn_sample_call(q_bf, kn_bf, vn_bf, pool_k, pool_v, page_table, bias_last, bias_new, cfar,
                      lam_p, g, lam_init):
    bs, tq, d = q_bf.shape
    n_pages_total = page_table.shape[1]
    pp = PAGES_PER_STEP
    n_chunks = n_pages_total // pp
    tok = pl.BlockSpec((None, tq, d), lambda b, c, pt: (b, 0, 0))

    def page_spec(j):
        return pl.BlockSpec((None, PAGE_SIZE * ATT_HEADS, LANES), lambda b, c, pt: (pt[b, c * pp + j], 0, 0))

    def const(shape):
        n = len(shape)
        return pl.BlockSpec(shape, lambda b, c, pt: (0,) * n)

    grid_spec = pltpu.PrefetchScalarGridSpec(
        num_scalar_prefetch=1,
        grid=(bs, n_chunks),
        in_specs=[tok, tok, tok]
        + [page_spec(j) for j in range(pp)]
        + [page_spec(j) for j in range(pp)]
        + [const(bias_last.shape), const(bias_new.shape), pl.BlockSpec(memory_space=pltpu.SMEM),
           const((4, ATT_HD)), const((1, ATT_VD))],
        out_specs=tok,
        scratch_shapes=[
            pltpu.VMEM((ATT_HEADS, pp * PAGE_SIZE, LANES), BF16),
            pltpu.VMEM((ATT_HEADS, pp * PAGE_SIZE, LANES), BF16),
            pltpu.VMEM((ATT_HEADS, 2 * tq, 1), F32),
            pltpu.VMEM((ATT_HEADS, 2 * tq, 1), F32),
            pltpu.VMEM((ATT_HEADS, 2 * tq, ATT_VD), F32),
        ],
    )
    return pl.pallas_call(
        functools.partial(_attn_sample_kernel, tq=tq, n_pages=pp, lam_init=lam_init),
        grid_spec=grid_spec,
        out_shape=jax.ShapeDtypeStruct((bs, tq, d), BF16),
        compiler_params=_cparams("arbitrary", "arbitrary"),
        name="attn_sample",
    )(page_table, q_bf, kn_bf, vn_bf, *([pool_k] * pp), *([pool_v] * pp), bias_last, bias_new, cfar,
      lam_p, g.reshape(1, ATT_VD))


def _bias_by_distance(rel_bias, n):
    max_exact = REL_BUCKETS // 2
    nf = jnp.maximum(n, 1).astype(F32)
    large = max_exact + (jnp.log(nf / max_exact) / math.log(REL_MAX_DIST / max_exact)
                         * (REL_BUCKETS - max_exact)).astype(jnp.int32)
    bucket = jnp.where(n < max_exact, n, jnp.minimum(large, REL_BUCKETS - 1))
    return rel_bias.astype(F32).T[:, bucket]


def _toeplitz_bias(rel_bias, q_pos, k_pos):
    dist = q_pos[:, None] - k_pos[None, :]
    bias = _bias_by_distance(rel_bias, jnp.maximum(dist, 0))
    return jnp.where((dist >= 0)[None], bias, NEG_BIG)


def _oproj_kernel(a_ref, w_ref, x_ref, gate_ref, g_ref, b_ref, o_ref):
    out = jnp.dot(a_ref[...], w_ref[...], preferred_element_type=F32)
    r = DEEPNORM_ALPHA * x_ref[...] + gate_ref[...] * out
    o_ref[...] = _layer_norm(r, g_ref[...], b_ref[...])


def _oproj_call(a_bf, w_bf, x, mod, ln_g, ln_b):
    bx, tx, d = x.shape
    dk = a_bf.shape[-1]
    tm = min(512, tx)
    _, _, gate = _mod_specs(mod.shape[1], tm)
    tok = pl.BlockSpec((None, tm, d), lambda b, t: (b, t, 0))
    return pl.pallas_call(
        _oproj_kernel,
        grid=(bx, tx // tm),
        in_specs=[pl.BlockSpec((None, tm, dk), lambda b, t: (b, t, 0)), _const_spec(w_bf.shape), tok, gate,
                  _const_spec((1, d)), _const_spec((1, d))],
        out_specs=tok,
        out_shape=jax.ShapeDtypeStruct((bx, tx, d), F32),
        compiler_params=_cparams("arbitrary", "arbitrary"),
        name="attn_out_proj",
    )(a_bf, w_bf, x, mod, ln_g.reshape(1, d), ln_b.reshape(1, d))


def _ffn_kernel(x_ref, sh_ref, sc_ref, gate_ref, wi_ref, wo_ref, g_ref, b_ref, o_ref, acc_ref):
    x = x_ref[...]
    h = (x * (1.0 + sc_ref[...]) + sh_ref[...]).astype(BF16)
    for c in range(FFN_HIDDEN // FFN_CHUNK):
        lo = c * FFN_CHUNK
        gte = jnp.dot(h, wi_ref[:, lo:lo + FFN_CHUNK], preferred_element_type=F32)
        up = jnp.dot(h, wi_ref[:, FFN_HIDDEN + lo:FFN_HIDDEN + lo + FFN_CHUNK], preferred_element_type=F32)
        a = (_silu(gte) * up).astype(BF16)
        part = jnp.dot(a, wo_ref[lo:lo + FFN_CHUNK, :], preferred_element_type=F32)
        if c == 0:
            acc_ref[...] = part
        else:
            acc_ref[...] += part
    r = DEEPNORM_ALPHA * x + gate_ref[...] * acc_ref[...]
    o_ref[...] = _layer_norm(r, g_ref[...], b_ref[...])


def _ffn_call(x, mod, wi_bf, wo_bf, ln_g, ln_b):
    bx, tx, d = x.shape
    tm = min(512, tx)
    sh, sc, gate = _mod_specs(mod.shape[1], tm)
    tok = pl.BlockSpec((None, tm, d), lambda b, t: (b, t, 0))
    return pl.pallas_call(
        _ffn_kernel,
        grid=(bx, tx // tm),
        in_specs=[tok, sh, sc, gate, _const_spec(wi_bf.shape), _const_spec(wo_bf.shape),
                  _const_spec((1, d)), _const_spec((1, d))],
        out_specs=tok,
        out_shape=jax.ShapeDtypeStruct((bx, tx, d), F32),
        scratch_shapes=[pltpu.VMEM((tm, d), F32)],
        compiler_params=_cparams("arbitrary", "arbitrary"),
        name="ffn",
    )(x, mod, mod, mod, wi_bf, wo_bf, ln_g.reshape(1, d), ln_b.reshape(1, d))


def _ssd_in_kernel(x_ref, sh_ref, sc_ref, w_ref, z_ref, xbc_ref, dt_ref):
    h = (x_ref[...] * (1.0 + sc_ref[...]) + sh_ref[...]).astype(BF16)
    di = SSD_D_INNER
    for lo in range(0, di, 1024):
        z_ref[:, lo:lo + 1024] = jnp.dot(h, w_ref[:, lo:lo + 1024], preferred_element_type=F32)
    for lo in range(0, SSD_CONV_DIM, 1024):
        xbc_ref[:, lo:lo + 1024] = jnp.dot(h, w_ref[:, di + lo:di + lo + 1024], preferred_element_type=F32)
    lo = di + SSD_CONV_DIM
    dt_ref[...] = jnp.dot(h, w_ref[:, lo:lo + DT_PAD], preferred_element_type=F32)


def _ssd_in_call(x, mod, w_bf):
    bx, tx, d = x.shape
    tm = min(256, tx)
    sh, sc, _ = _mod_specs(mod.shape[1], tm)

    def tok(width):
        return pl.BlockSpec((None, tm, width), lambda b, t: (b, t, 0))

    return pl.pallas_call(
        _ssd_in_kernel,
        grid=(bx, tx // tm),
        in_specs=[tok(d), sh, sc, _const_spec(w_bf.shape)],
        out_specs=[tok(SSD_D_INNER), tok(SSD_CONV_DIM), tok(DT_PAD)],
        out_shape=[
            jax.ShapeDtypeStruct((bx, tx, SSD_D_INNER), F32),
            jax.ShapeDtypeStruct((bx, tx, SSD_CONV_DIM), F32),
            jax.ShapeDtypeStruct((bx, tx, DT_PAD), F32),
        ],
        compiler_params=_cparams("arbitrary", "arbitrary"),
        name="ssd_in_proj",
    )(x, mod, mod, w_bf)


def _ssd_scan_kernel(xbc_ref, dt_ref, conv0_ref, ssm0_ref, cw_ref, cb_ref, dtb_ref, alog_ref, dsk_ref,
                     y_ref, fin_ref, nconv_ref, ext_sc, xs_sc, st_sc, *, rows):
    L = SSD_CHUNK
    di = SSD_D_INNER
    c = pl.program_id(1)
    last = pl.num_programs(1) - 1
    tail = 8

    @pl.when(c == 0)
    def _():
        ext_sc[0:tail, :] = jnp.zeros((tail, SSD_CONV_DIM), F32)
        ext_sc[tail - (SSD_CONV - 1):tail, :] = conv0_ref[...]
        st_sc[...] = ssm0_ref[...]

    ext_sc[tail:tail + rows, :] = xbc_ref[...]
    if rows < L:
        ext_sc[tail + rows:tail + L, :] = jnp.zeros((L - rows, SSD_CONV_DIM), F32)

    @pl.when(c == last)
    def _():
        nconv_ref[...] = ext_sc[tail + rows - (SSD_CONV - 1):tail + rows, :]

    cblk = 512
    for lo in range(0, SSD_CONV_DIM, cblk):
        acc = cb_ref[:, lo:lo + cblk] + ext_sc[tail:tail + L, lo:lo + cblk] * cw_ref[3:4, lo:lo + cblk]
        for k in range(1, SSD_CONV):
            w_row = cw_ref[3 - k:4 - k, lo:lo + cblk]
            acc = acc + ext_sc[tail - k:tail - k + L, lo:lo + cblk] * w_row
        xs_sc[:, lo:lo + cblk] = _silu(acc)

    ext_sc[0:tail, :] = ext_sc[L:L + tail, :]

    if rows < L:
        dt_raw = jnp.concatenate([dt_ref[...], jnp.zeros((L - rows, DT_PAD), F32)], axis=0)
    else:
        dt_raw = dt_ref[...]
    xx = dt_raw + dtb_ref[...]
    dt = jnp.maximum(xx, 0.0) + jnp.log1p(jnp.exp(-jnp.abs(xx)))
    if rows < L:
        row = lax.broadcasted_iota(jnp.int32, (L, DT_PAD), 0)
        dt = jnp.where(row < rows, dt, 0.0)
    adt = -jnp.exp(alog_ref[...]) * dt
    ri = lax.broadcasted_iota(jnp.int32, (L, L), 0)
    ci = lax.broadcasted_iota(jnp.int32, (L, L), 1)
    causal = ci <= ri
    tril = jnp.where(causal, 1.0, 0.0).astype(F32)
    acs = jnp.dot(tril, adt, preferred_element_type=F32, precision=lax.Precision.HIGHEST)
    acs_t = acs.T
    dt_t = dt.T
    acs_end_t = acs_t[:, L - 1:L]
    w_t = dt_t * jnp.exp(acs_end_t - acs_t)
    cdec_t = jnp.exp(acs_end_t)

    half = lax.broadcasted_iota(jnp.int32, (L, LANES), 1) < SSD_HEADDIM
    for g in range(SSD_GROUPS):
        bm = xs_sc[:, di + g * SSD_STATE:di + (g + 1) * SSD_STATE]
        cm = xs_sc[:, di + SSD_GN + g * SSD_STATE:di + SSD_GN + (g + 1) * SSD_STATE]
        bm_bf = bm.astype(BF16)
        cb = _nt_dot(cm.astype(BF16), bm_bf)
        for jj in range(SSD_HEADS // SSD_GROUPS // 2):
            j = g * (SSD_HEADS // SSD_GROUPS // 2) + jj
            ps = slice(j * LANES, (j + 1) * LANES)
            x_pair = xs_sc[:, ps]
            x_bf = x_pair.astype(BF16)
            st_pair = st_sc[ps, :]
            st_bf = st_pair.astype(BF16)
            ys = []
            for hh in range(2):
                hd = 2 * j + hh
                acs_col = jnp.broadcast_to(acs[:, hd:hd + 1], (L, L))
                seg = acs_col - acs_t[hd:hd + 1, :]
                dec = jnp.exp(jnp.where(causal, seg, -jnp.inf))
                mh = (cb * dec * dt_t[hd:hd + 1, :]).astype(BF16)
                y_h = jnp.dot(mh, x_bf, preferred_element_type=F32)
                cme = (cm * jnp.exp(acs_col)).astype(BF16)
                y_h = y_h + _nt_dot(cme, st_bf)
                ys.append(y_h)
            y_pair = jnp.where(half, ys[0], ys[1]) + x_pair * dsk_ref[:, ps]
            y_ref[:, ps] = y_pair[0:rows]
            x_t = x_pair.T
            w_rows = jnp.concatenate([jnp.broadcast_to(w_t[2 * j:2 * j + 1, :], (SSD_HEADDIM, L)),
                                      jnp.broadcast_to(w_t[2 * j + 1:2 * j + 2, :], (SSD_HEADDIM, L))], axis=0)
            d_rows = jnp.concatenate([jnp.broadcast_to(cdec_t[2 * j:2 * j + 1, :], (SSD_HEADDIM, 1)),
                                      jnp.broadcast_to(cdec_t[2 * j + 1:2 * j + 2, :], (SSD_HEADDIM, 1))], axis=0)
            upd = jnp.dot((x_t * w_rows).astype(BF16), bm_bf, preferred_element_type=F32)
            st_sc[ps, :] = st_pair * d_rows + upd

    @pl.when(c == last)
    def _():
        fin_ref[...] = st_sc[...]


def _ssd_scan_call(xbc, dt, conv0, ssm0, conv_w, conv_b, dt_bias, a_log, d_skip):
    bs, tt, _ = xbc.shape
    rows = SSD_CHUNK if tt % SSD_CHUNK == 0 else tt
    nc = tt // rows
    heads_pad = DT_PAD - SSD_HEADS
    dtb = jnp.pad(dt_bias.astype(F32), (0, heads_pad)).reshape(1, DT_PAD)
    alog = jnp.pad(a_log.astype(F32), (0, heads_pad)).reshape(1, DT_PAD)
    dsk = jnp.repeat(d_skip.astype(F32), SSD_HEADDIM).reshape(1, SSD_D_INNER)

    def tok(width):
        return pl.BlockSpec((None, rows, width), lambda b, c: (b, c, 0))

    def per_seq(r, width):
        return pl.BlockSpec((None, r, width), lambda b, c: (b, 0, 0))

    return pl.pallas_call(
        functools.partial(_ssd_scan_kernel, rows=rows),
        grid=(bs, nc),
        in_specs=[tok(SSD_CONV_DIM), tok(DT_PAD), per_seq(SSD_CONV - 1, SSD_CONV_DIM),
                  per_seq(SSD_D_INNER, SSD_STATE),
                  _const_spec((SSD_CONV, SSD_CONV_DIM)), _const_spec((1, SSD_CONV_DIM)),
                  _const_spec((1, DT_PAD)), _const_spec((1, DT_PAD)), _const_spec((1, SSD_D_INNER))],
        out_specs=[tok(SSD_D_INNER), per_seq(SSD_D_INNER, SSD_STATE), per_seq(SSD_CONV - 1, SSD_CONV_DIM)],
        out_shape=[
            jax.ShapeDtypeStruct((bs, tt, SSD_D_INNER), F32),
            jax.ShapeDtypeStruct((bs, SSD_D_INNER, SSD_STATE), F32),
            jax.ShapeDtypeStruct((bs, SSD_CONV - 1, SSD_CONV_DIM), F32),
        ],
        scratch_shapes=[
            pltpu.VMEM((SSD_CHUNK + 8, SSD_CONV_DIM), F32),
            pltpu.VMEM((SSD_CHUNK, SSD_CONV_DIM), F32),
            pltpu.VMEM((SSD_D_INNER, SSD_STATE), F32),
        ],
        compiler_params=_cparams("arbitrary", "arbitrary"),
        name="ssd_scan",
    )(xbc, dt, conv0, ssm0, conv_w, conv_b.reshape(1, SSD_CONV_DIM), dtb, alog, dsk)


def _ssd_out_kernel(y_ref, z_ref, ng_ref, w_ref, x_ref, gate_ref, g_ref, b_ref, o_ref, a_sc):
    gw = SSD_D_INNER // SSD_GROUPS
    for g in range(SSD_GROUPS):
        gs = slice(g * gw, (g + 1) * gw)
        yz = y_ref[:, gs] * _silu(z_ref[:, gs])
        ms = jnp.mean(yz * yz, axis=-1, keepdims=True)
        a_sc[:, gs] = (yz * lax.rsqrt(ms + RMS_EPS) * ng_ref[:, gs]).astype(BF16)
    out = jnp.dot(a_sc[...], w_ref[...], preferred_element_type=F32)
    r = DEEPNORM_ALPHA * x_ref[...] + gate_ref[...] * out
    o_ref[...] = _layer_norm(r, g_ref[...], b_ref[...])


def _ssd_out_call(y, z, norm_g, w_bf, x, mod, ln_g, ln_b):
    bx, tx, d = x.shape
    tm = min(512, tx)
    _, _, gate = _mod_specs(mod.shape[1], tm)

    def tok(width):
        return pl.BlockSpec((None, tm, width), lambda b, t: (b, t, 0))

    return pl.pallas_call(
        _ssd_out_kernel,
        grid=(bx, tx // tm),
        in_specs=[tok(SSD_D_INNER), tok(SSD_D_INNER), _const_spec((1, SSD_D_INNER)), _const_spec(w_bf.shape),
                  tok(d), gate, _const_spec((1, d)), _const_spec((1, d))],
        out_specs=tok(d),
        out_shape=jax.ShapeDtypeStruct((bx, tx, d), F32),
        scratch_shapes=[pltpu.VMEM((tm, SSD_D_INNER), BF16)],
        compiler_params=_cparams("arbitrary", "arbitrary"),
        name="ssd_out_proj",
    )(y, z, norm_g.reshape(1, SSD_D_INNER), w_bf, x, mod, ln_g.reshape(1, d), ln_b.reshape(1, d))


def _attention_layer(j, lam_init, y_p, y_s, mod_p, mod_s, cache_k, cache_v, page_table, rel_bias,
                     attn_w_qkv, attn_lambda, attn_subln_g, attn_w_o, ln_g, ln_b):
    b, t, d = y_p.shape
    w_qkv = attn_w_qkv[j].astype(BF16)
    w_o = attn_w_o[j].astype(BF16)
    lam_p = attn_lambda[j].astype(F32)
    g = attn_subln_g[j].astype(F32)

    tile = min(ATTN_TILE, t)
    pos = jnp.arange(tile)
    bias_tiles = jnp.stack([_toeplitz_bias(rel_bias, pos, pos), _toeplitz_bias(rel_bias, pos + tile, pos)], axis=1)
    cfar = _bias_by_distance(rel_bias, jnp.asarray([2 * tile], jnp.int32))[:, 0]
    q_bf, k_p, v_p, k_bf, v_bf = _qkv_call(y_p, mod_p, w_qkv)
    o_p = _attn_prompt_call(q_bf, k_bf, v_bf, bias_tiles, cfar, lam_p, g, lam_init)
    y_p = _oproj_call(o_p, w_o, y_p, mod_p, ln_g, ln_b)

    n_seq, tq = page_table.shape[0], y_s.shape[1] // page_table.shape[0]
    past = page_table.shape[1] * PAGE_SIZE
    span = PAGES_PER_STEP * PAGE_SIZE
    q_pos = past + jnp.arange(tq)
    bias_last = _toeplitz_bias(rel_bias, q_pos, past - span + jnp.arange(span))
    new_pos = past + jnp.arange(LANES)
    bias_new = _toeplitz_bias(rel_bias, q_pos, new_pos)
    bias_new = jnp.where((jnp.arange(LANES) < tq)[None, None, :], bias_new, NEG_BIG)
    bias_last = jnp.concatenate([bias_last, bias_last], axis=1)
    bias_new = jnp.concatenate([bias_new, bias_new], axis=1)
    q_bf, k_s, v_s, k_bf, v_bf = _qkv_call(y_s, mod_s, w_qkv)
    seq = lambda a: a.reshape(n_seq, tq, d)
    pool_k = cache_k[j].reshape(cache_k.shape[1], PAGE_SIZE * ATT_HEADS, LANES)
    pool_v = cache_v[j].reshape(cache_v.shape[1], PAGE_SIZE * ATT_HEADS, LANES)
    o_s = _attn_sample_call(seq(q_bf), seq(k_bf), seq(v_bf), pool_k, pool_v, page_table, bias_last, bias_new,
                            cfar, lam_p, g, lam_init)
    y_s = _oproj_call(o_s.reshape(1, n_seq * tq, d), w_o, y_s, mod_s, ln_g, ln_b)

    kv = lambda a, n, tt: a.reshape(n, tt, ATT_HEADS, 2 * ATT_HD)
    return y_p, y_s, kv(k_p, b, t), kv(v_p, b, t), kv(k_s, n_seq, tq), kv(v_s, n_seq, tq)


def _ssd_layer(j, y_p, y_s, mod_p, mod_s, n_seq, state_ssm, state_conv, ssd_w_in, ssd_conv_w, ssd_conv_b,
               ssd_dt_bias, ssd_a_log, ssd_d, ssd_norm_g, ssd_w_out, ln_g, ln_b):
    b, t, d = y_p.shape
    tq = y_s.shape[1] // n_seq
    w_in = jnp.pad(ssd_w_in[j], ((0, 0), (0, DT_PAD - SSD_HEADS))).astype(BF16)
    w_out = ssd_w_out[j].astype(BF16)
    scan_w = (ssd_conv_w[j].astype(F32), ssd_conv_b[j].astype(F32), ssd_dt_bias[j], ssd_a_log[j], ssd_d[j])

    def mixer(y, mod, seqs, rows, conv0, ssm0):
        z, xbc, dt = _ssd_in_call(y, mod, w_in)
        per_seq = lambda a: a.reshape(seqs, rows, a.shape[-1])
        yy, fin, nconv = _ssd_scan_call(per_seq(xbc), per_seq(dt), conv0, ssm0, *scan_w)
        yy = yy.reshape(y.shape[0], y.shape[1], SSD_D_INNER)
        y_new = _ssd_out_call(yy, z, ssd_norm_g[j].astype(F32), w_out, y, mod, ln_g, ln_b)
        return y_new, fin.reshape(seqs, SSD_HEADS, SSD_HEADDIM, SSD_STATE), nconv

    zero_conv = jnp.zeros((b, SSD_CONV - 1, SSD_CONV_DIM), F32)
    zero_ssm = jnp.zeros((b, SSD_D_INNER, SSD_STATE), F32)
    y_p, ssm_p, conv_p = mixer(y_p, mod_p, b, t, zero_conv, zero_ssm)
    y_s, ssm_s, conv_s = mixer(y_s, mod_s, n_seq, tq, state_conv[j].astype(F32),
                               state_ssm[j].astype(F32).reshape(n_seq, SSD_D_INNER, SSD_STATE))
    return y_p, y_s, ssm_p, conv_p, ssm_s, conv_s


def kernel(x_prompt, x_sample, cache_k, cache_v, state_ssm, state_conv, page_table, c_prompt, c_sample,
           ada_w, ada_b, ln_g, ln_b, rel_bias, attn_w_qkv, attn_lambda, attn_subln_g, attn_w_o,
           ssd_w_in, ssd_conv_w, ssd_conv_b, ssd_dt_bias, ssd_a_log, ssd_d, ssd_norm_g, ssd_w_out,
           ffn_w_in, ffn_w_out):
    b, t, d = x_prompt.shape
    n_seq, tq, _ = x_sample.shape
    depth = ada_w.shape[0]

    n_c = b + n_seq
    rc = -(-n_c // 8) * 8
    c_all = jnp.concatenate([c_prompt, c_sample, jnp.zeros((rc - n_c, d), F32)], axis=0)
    mod_all = _mod_call(c_all, ada_w.reshape(depth * 2, d, 3 * d), ada_b.reshape(depth * 2, 3 * d))

    def mods(i, s):
        m = mod_all[2 * i + s]
        mod_p = m[0:b].reshape(b, 1, 3 * d)
        mod_s = jnp.repeat(m[b:b + n_seq], tq, axis=0).reshape(1, n_seq * tq, 3 * d)
        return mod_p, mod_s

    y_p = x_prompt
    y_s = x_sample.reshape(1, n_seq * tq, d)
    k_p, v_p, k_s, v_s = [], [], [], []
    ssm_p, conv_p, ssm_s, conv_s = [], [], [], []
    for i in range(depth):
        j = i // 2
        mod_p, mod_s = mods(i, 0)
        if i % 2 == 0:
            lam_init = 0.8 - 0.6 * math.exp(-0.3 * i)
            y_p, y_s, kp, vp, ks, vs = _attention_layer(
                j, lam_init, y_p, y_s, mod_p, mod_s, cache_k, cache_v, page_table, rel_bias,
                attn_w_qkv, attn_lambda, attn_subln_g, attn_w_o, ln_g[i, 0], ln_b[i, 0])
            k_p.append(kp)
            v_p.append(vp)
            k_s.append(ks)
            v_s.append(vs)
        else:
            y_p, y_s, sp, cp, ss, cs = _ssd_layer(
                j, y_p, y_s, mod_p, mod_s, n_seq, state_ssm, state_conv, ssd_w_in, ssd_conv_w, ssd_conv_b,
                ssd_dt_bias, ssd_a_log, ssd_d, ssd_norm_g, ssd_w_out, ln_g[i, 0], ln_b[i, 0])
            ssm_p.append(sp)
            conv_p.append(cp)
            ssm_s.append(ss)
            conv_s.append(cs)
        mod_p, mod_s = mods(i, 1)
        wi = ffn_w_in[i].astype(BF16)
        wo = ffn_w_out[i].astype(BF16)
        y_p = _ffn_call(y_p, mod_p, wi, wo, ln_g[i, 1], ln_b[i, 1])
        y_s = _ffn_call(y_s, mod_s, wi, wo, ln_g[i, 1], ln_b[i, 1])
    return (y_p, y_s.reshape(n_seq, tq, d), jnp.stack(k_p), jnp.stack(v_p), jnp.stack(k_s), jnp.stack(v_s),
            jnp.stack(ssm_p), jnp.stack(conv_p), jnp.stack(ssm_s), jnp.stack(conv_s))
```

```python
import functools
import math

import jax
import jax.numpy as jnp
from jax import lax
from jax.experimental import pallas as pl
from jax.experimental.pallas import tpu as pltpu

F32 = jnp.float32
BF16 = jnp.bfloat16

D_MODEL = 1024
DEPTH = 4
PAGE_SIZE = 128
ATT_HEADS = 8
ATT_HD = 64
ATT_VD = 128
REL_BUCKETS = 32
REL_MAX_DIST = 128
SSD_D_INNER = 2048
SSD_HEADDIM = 64
SSD_HEADS = 32
SSD_GROUPS = 4
SSD_STATE = 128
SSD_CONV = 4
SSD_GN = SSD_GROUPS * SSD_STATE
SSD_CONV_DIM = SSD_D_INNER + 2 * SSD_GN
SSD_CHUNK = 128
FFN_HIDDEN = 2816
DEEPNORM_ALPHA = (2 * DEPTH) ** 0.25
LN_EPS = 1e-5
RMS_EPS = 1e-5

LANES = 128
DT_PAD = LANES
NEG_BIG = -1e30
LOG2E = math.log2(math.e)
VMEM_LIMIT = 56 * 1024 * 1024
ATTN_TILE = 512
ATTN_ROWS = 256
PAGES_PER_STEP = 8
FFN_CHUNK = 256


def _cparams(*sem):
    return pltpu.CompilerParams(dimension_semantics=sem, vmem_limit_bytes=VMEM_LIMIT)


def _silu(x):
    return x * jax.nn.sigmoid(x)


def _layer_norm(r, g, b):
    mu = jnp.mean(r, axis=-1, keepdims=True)
    d = r - mu
    var = jnp.mean(d * d, axis=-1, keepdims=True)
    return d * lax.rsqrt(var + LN_EPS) * g + b


def _mod_specs(mod_rows, tm):
    rows = 1 if mod_rows == 1 else tm

    def spec(piece):
        if mod_rows == 1:
            return pl.BlockSpec((None, 1, D_MODEL), lambda b, t: (b, 0, piece))
        return pl.BlockSpec((None, rows, D_MODEL), lambda b, t: (b, t, piece))

    return spec(0), spec(1), spec(2)


def _const_spec(shape):
    n = len(shape)
    return pl.BlockSpec(shape, lambda b, t: (0,) * n)


def _mod_kernel(c_ref, w_ref, b_ref, o_ref):
    a = _silu(c_ref[...]).astype(BF16)
    o_ref[...] = jnp.dot(a, w_ref[...].astype(BF16), preferred_element_type=F32) + b_ref[...]


def _mod_call(c_all, ada_w, ada_b):
    n_sub, d, d3 = ada_w.shape
    rc = c_all.shape[0]
    tn = 1024
    return pl.pallas_call(
        _mod_kernel,
        grid=(n_sub, d3 // tn),
        in_specs=[
            pl.BlockSpec((rc, d), lambda i, j: (0, 0)),
            pl.BlockSpec((None, d, tn), lambda i, j: (i, 0, j)),
            pl.BlockSpec((None, 1, tn), lambda i, j: (i, 0, j)),
        ],
        out_specs=pl.BlockSpec((None, rc, tn), lambda i, j: (i, 0, j)),
        out_shape=jax.ShapeDtypeStruct((n_sub, rc, d3), F32),
        compiler_params=_cparams("arbitrary", "arbitrary"),
        name="adaln_mod",
    )(c_all, ada_w, ada_b.reshape(n_sub, 1, d3))


def _qkv_kernel(x_ref, sh_ref, sc_ref, w_ref, q_ref, k_ref, v_ref, kb_ref, vb_ref):
    d = D_MODEL
    h = (x_ref[...] * (1.0 + sc_ref[...]) + sh_ref[...]).astype(BF16)
    q = jnp.dot(h, w_ref[:, 0:d], preferred_element_type=F32)
    q_ref[...] = (q * (ATT_HD ** -0.5 * LOG2E)).astype(BF16)
    tm = x_ref.shape[0]
    k = jnp.dot(h, w_ref[:, d:2 * d], preferred_element_type=F32)
    kb_ref[...] = k.astype(BF16)
    v = jnp.dot(h, w_ref[:, 2 * d:3 * d], preferred_element_type=F32)
    vb_ref[...] = v.astype(BF16)
    for hd in range(ATT_HEADS):
        hs = slice(hd * LANES, (hd + 1) * LANES)
        k_ref[pl.ds(hd, tm, stride=ATT_HEADS), :] = k[:, hs]
        v_ref[pl.ds(hd, tm, stride=ATT_HEADS), :] = v[:, hs]


def _qkv_call(x, mod, w_bf):
    bx, tx, d = x.shape
    tm = min(512, tx)
    sh, sc, _ = _mod_specs(mod.shape[1], tm)
    tok = pl.BlockSpec((None, tm, d), lambda b, t: (b, t, 0))
    rows = pl.BlockSpec((None, tm * ATT_HEADS, LANES), lambda b, t: (b, t, 0))
    return pl.pallas_call(
        _qkv_kernel,
        grid=(bx, tx // tm),
        in_specs=[tok, sh, sc, _const_spec(w_bf.shape)],
        out_specs=[tok, rows, rows, tok, tok],
        out_shape=[
            jax.ShapeDtypeStruct((bx, tx, d), BF16),
            jax.ShapeDtypeStruct((bx, tx * ATT_HEADS, LANES), F32),
            jax.ShapeDtypeStruct((bx, tx * ATT_HEADS, LANES), F32),
            jax.ShapeDtypeStruct((bx, tx, d), BF16),
            jax.ShapeDtypeStruct((bx, tx, d), BF16),
        ],
        compiler_params=_cparams("arbitrary", "arbitrary"),
        name="qkv_proj",
    )(x, mod, mod, w_bf)


def _stack_maps(q):
    lane = lax.broadcasted_iota(jnp.int32, q.shape, 1)
    zero = jnp.zeros_like(q)
    return jnp.concatenate([jnp.where(lane < ATT_HD, q, zero), jnp.where(lane >= ATT_HD, q, zero)], axis=0)


def _lambda_full(lam_ref, lam_init):
    lp = lam_ref[...]
    a = jnp.sum(lp[0:1, :] * lp[1:2, :], axis=-1, keepdims=True)
    b = jnp.sum(lp[2:3, :] * lp[3:4, :], axis=-1, keepdims=True)
    return jnp.exp(a) - jnp.exp(b) + lam_init


def _diff_finish(num, den, t, lam, g, lam_init):
    o = num[0:t] / den[0:t] - lam * (num[t:2 * t] / den[t:2 * t])
    ms = jnp.mean(o * o, axis=-1, keepdims=True)
    return o * lax.rsqrt(ms + RMS_EPS) * g * (1.0 - lam_init)


def _nt_dot(a, b):
    return lax.dot_general(a, b, (((1,), (1,)), ((), ())), preferred_element_type=F32)


def _lane_tile(x, n):
    return jnp.concatenate([x] * n, axis=1)


def _online_softmax(s, shift_bias, v_ext, m_prev, acc_prev):
    m_curr = jnp.max(s, axis=1, keepdims=True)
    if shift_bias is not None:
        m_curr = m_curr + shift_bias
    m_next = jnp.maximum(m_prev, m_curr)
    shift = m_next if shift_bias is None else m_next - shift_bias
    p = jnp.exp2(s - _lane_tile(shift, s.shape[1] // LANES))
    alpha = jnp.exp2(m_prev - m_next)
    acc = _lane_tile(alpha, 2) * acc_prev + jnp.dot(p.astype(BF16), v_ext, preferred_element_type=F32)
    return m_next, acc


def _attn_prompt_kernel(q_ref, k_ref, v_ref, bias_ref, cfar_ref, lam_ref, g_ref, o_ref,
                        qq_sc, vext_sc, m_sc, acc_sc, *, t, lam_init):
    h = pl.program_id(1)
    qi = pl.program_id(2)
    rs = min(ATTN_ROWS, t)

    @pl.when(qi == 0)
    def _():
        vext_sc[:, 0:ATT_VD] = v_ref[...]
        vext_sc[:, ATT_VD:2 * ATT_VD] = jnp.ones((vext_sc.shape[0], ATT_VD), BF16)

    qq_sc[...] = _stack_maps(q_ref[...])
    m_sc[...] = jnp.full(m_sc.shape, NEG_BIG, F32)
    acc_sc[...] = jnp.zeros(acc_sc.shape, F32)
    cfar = cfar_ref[h]

    def tile_update(ki, bias_idx):
        start = pl.multiple_of(ki * t, t)
        k = k_ref[pl.ds(start, t), :]
        v_ext = vext_sc[pl.ds(start, t), :]
        for r in range(2 * t // rs):
            rows = slice(r * rs, (r + 1) * rs)
            s = _nt_dot(qq_sc[rows, :], k)
            if bias_idx is None:
                m, acc = _online_softmax(s, cfar, v_ext, m_sc[rows, :], acc_sc[rows, :])
            else:
                b0 = (r * rs) % t
                s = s + bias_ref[bias_idx, b0:b0 + rs, :]
                m, acc = _online_softmax(s, None, v_ext, m_sc[rows, :], acc_sc[rows, :])
            m_sc[rows, :] = m
            acc_sc[rows, :] = acc

    def far_body(ki, carry):
        tile_update(ki, None)
        return carry

    lax.fori_loop(0, qi - 1, far_body, 0)

    @pl.when(qi >= 1)
    def _():
        tile_update(qi - 1, 1)

    tile_update(qi, 0)
    lam = _lambda_full(lam_ref, lam_init)
    acc = acc_sc[...]
    o_ref[...] = _diff_finish(acc[:, 0:ATT_VD], acc[:, ATT_VD:2 * ATT_VD], t, lam, g_ref[...], lam_init).astype(BF16)


def _attn_prompt_call(q_bf, k_bf, v_bf, bias_tiles, cfar, lam_p, g, lam_init):
    b, tt, d = q_bf.shape
    t = bias_tiles.shape[-1]
    qtile = pl.BlockSpec((None, t, LANES), lambda b, h, q: (b, q, h))
    head = pl.BlockSpec((None, tt, LANES), lambda b, h, q: (b, 0, h))
    return pl.pallas_call(
        functools.partial(_attn_prompt_kernel, t=t, lam_init=lam_init),
        grid=(b, ATT_HEADS, tt // t),
        in_specs=[
            qtile, head, head,
            pl.BlockSpec((None, 2, t, t), lambda b, h, q: (h, 0, 0, 0)),
            pl.BlockSpec(memory_space=pltpu.SMEM),
            pl.BlockSpec((4, ATT_HD), lambda b, h, q: (0, 0)),
            pl.BlockSpec((1, ATT_VD), lambda b, h, q: (0, 0)),
        ],
        out_specs=qtile,
        out_shape=jax.ShapeDtypeStruct((b, tt, d), BF16),
        scratch_shapes=[
            pltpu.VMEM((2 * t, LANES), BF16),
            pltpu.VMEM((tt, 2 * ATT_VD), BF16),
            pltpu.VMEM((2 * t, LANES), F32),
            pltpu.VMEM((2 * t, 2 * ATT_VD), F32),
        ],
        compiler_params=_cparams("arbitrary", "arbitrary", "arbitrary"),
        name="attn_prompt",
    )(q_bf, k_bf, v_bf, bias_tiles, cfar, lam_p, g.reshape(1, ATT_VD))


def _attn_sample_kernel(pt_ref, q_ref, kn_ref, vn_ref, *rest, tq, n_pages, lam_init):
    k_pages = rest[:n_pages]
    v_pages = rest[n_pages:2 * n_pages]
    (mask_far_ref, mask_last_ref, mask_new_ref, lam_ref, g_ref, o_ref,
     qq_sc, k_sc, v_sc, m_sc, l_sc, acc_sc) = rest[2 * n_pages:]
    c = pl.program_id(1)
    last = pl.num_programs(1) - 1
    page_rows = PAGE_SIZE * ATT_HEADS

    @pl.when(c == 0)
    def _():
        q = q_ref[...].astype(F32)
        for h in range(ATT_HEADS):
            qq_sc[h * 2 * tq:(h + 1) * 2 * tq, :] = _stack_maps(q[:, h * LANES:(h + 1) * LANES]).astype(BF16)
        m_sc[...] = jnp.full(m_sc.shape, NEG_BIG, F32)
        l_sc[...] = jnp.zeros(l_sc.shape, F32)
        acc_sc[...] = jnp.zeros(acc_sc.shape, F32)

    for j in range(n_pages):
        k_sc[j * page_rows:(j + 1) * page_rows, :] = k_pages[j][...].astype(BF16)
        v_sc[j * page_rows:(j + 1) * page_rows, :] = v_pages[j][...].astype(BF16)

    def update(s, v):
        m_prev = m_sc[...]
        m_next = jnp.maximum(m_prev, jnp.max(s, axis=1, keepdims=True))
        p = jnp.exp2(s - m_next)
        alpha = jnp.exp2(m_prev - m_next)
        l_sc[...] = alpha * l_sc[...] + jnp.sum(p, axis=1, keepdims=True)
        acc_sc[...] = alpha * acc_sc[...] + jnp.dot(p.astype(BF16), v, preferred_element_type=F32)
        m_sc[...] = m_next

    def past_logits():
        s = _nt_dot(qq_sc[...], k_sc[...])
        return s

    @pl.when(c < last)
    def _():
        update(past_logits() + _lane_tile(mask_far_ref[...], n_pages), v_sc[...])

    @pl.when(c == last)
    def _():
        update(past_logits() + mask_last_ref[...], v_sc[...])
        rows_new = kn_ref.shape[0]
        pad = jnp.zeros((LANES - rows_new, LANES), F32)
        kn = jnp.concatenate([kn_ref[...], pad], axis=0).astype(BF16)
        vn = jnp.concatenate([vn_ref[...], pad], axis=0).astype(BF16)
        update(_nt_dot(qq_sc[...], kn) + mask_new_ref[...], vn)
        lam = _lambda_full(lam_ref, lam_init)
        num = acc_sc[...]
        den = l_sc[...]
        for h in range(ATT_HEADS):
            rows = slice(h * 2 * tq, (h + 1) * 2 * tq)
            o_ref[:, h * LANES:(h + 1) * LANES] = _diff_finish(
                num[rows], den[rows], tq, lam, g_ref[...], lam_init).astype(BF16)


def _attn_sample_call(q_bf, k_new, v_new, pool_k, pool_v, pages, mask_far, mask_last, mask_new,
                      lam_p, g, lam_init):
    bs, tq, d = q_bf.shape
    n_pages_total = pages.shape[1]
    pp = PAGES_PER_STEP
    n_chunks = n_pages_total // pp
    page_rows = PAGE_SIZE * ATT_HEADS
    n_rows = ATT_HEADS * 2 * tq
    tok = pl.BlockSpec((None, tq, d), lambda b, c, pt: (b, 0, 0))
    new = pl.BlockSpec((None, tq * ATT_HEADS, LANES), lambda b, c, pt: (b, 0, 0))

    def page_spec(j):
        return pl.BlockSpec((None, page_rows, LANES), lambda b, c, pt: (pt[b, c * pp + j], 0, 0))

    def const(shape):
        n = len(shape)
        return pl.BlockSpec(shape, lambda b, c, pt: (0,) * n)

    grid_spec = pltpu.PrefetchScalarGridSpec(
        num_scalar_prefetch=1,
        grid=(bs, n_chunks),
        in_specs=[tok, new, new]
        + [page_spec(j) for j in range(pp)]
        + [page_spec(j) for j in range(pp)]
        + [const(mask_far.shape), const(mask_last.shape), const(mask_new.shape),
           const((4, ATT_HD)), const((1, ATT_VD))],
        out_specs=tok,
        scratch_shapes=[
            pltpu.VMEM((n_rows, LANES), BF16),
            pltpu.VMEM((pp * page_rows, LANES), BF16),
            pltpu.VMEM((pp * page_rows, LANES), BF16),
            pltpu.VMEM((n_rows, 1), F32),
            pltpu.VMEM((n_rows, 1), F32),
            pltpu.VMEM((n_rows, ATT_VD), F32),
        ],
    )
    return pl.pallas_call(
        functools.partial(_attn_sample_kernel, tq=tq, n_pages=pp, lam_init=lam_init),
        grid_spec=grid_spec,
        out_shape=jax.ShapeDtypeStruct((bs, tq, d), BF16),
        compiler_params=_cparams("arbitrary", "arbitrary"),
        name="attn_sample",
    )(pages, q_bf, k_new, v_new, *([pool_k] * pp), *([pool_v] * pp), mask_far, mask_last, mask_new,
      lam_p, g.reshape(1, ATT_VD))


def _bias_by_distance(rel_bias, n):
    max_exact = REL_BUCKETS // 2
    nn = jnp.maximum(n, 0)
    nf = jnp.maximum(nn, 1).astype(F32)
    large = max_exact + (jnp.log(nf / max_exact) / math.log(REL_MAX_DIST / max_exact)
                         * (REL_BUCKETS - max_exact)).astype(jnp.int32)
    bucket = jnp.where(nn < max_exact, nn, jnp.minimum(large, REL_BUCKETS - 1))
    table = (rel_bias.astype(F32) * LOG2E).T
    onehot = (bucket[None, :] == jnp.arange(REL_BUCKETS)[:, None]).astype(F32)
    vals = jnp.dot(table, onehot, precision=lax.Precision.HIGHEST)
    return jnp.where((n >= 0)[None, :], vals, NEG_BIG)


def _toeplitz_tile(rel_bias, t, offset):
    m = 2 * t
    k = jnp.arange(m)
    dist = jnp.where(k < t, offset - k, offset + m - k)
    v = _bias_by_distance(rel_bias, dist)
    flat = jnp.tile(v, (1, t))[:, :t * (m - 1)]
    return flat.reshape(-1, t, m - 1)[:, :, :t]


def _sample_masks(rel_bias, tq, past, span):
    h = ATT_HEADS
    same = (jnp.arange(h)[:, None] == jnp.arange(h)[None, :])
    cfar = _bias_by_distance(rel_bias, jnp.asarray([past], jnp.int32))[:, 0]

    def expand(base):
        n_tok = base.shape[-1]
        full = jnp.where(same[:, None, None, None, :], base[:, None, :, :, None], NEG_BIG)
        full = jnp.broadcast_to(full, (h, 2, tq, n_tok, h))
        return full.reshape(h * 2 * tq, n_tok * h)

    i = jnp.arange(tq)
    far = expand(jnp.broadcast_to(cfar[:, None, None], (h, tq, PAGE_SIZE)))
    j = jnp.arange(span)
    d_last = (span + i[:, None] - j[None, :]).reshape(-1)
    last = expand(_bias_by_distance(rel_bias, d_last).reshape(h, tq, span))
    n_new = LANES // h
    jn = jnp.arange(n_new)
    d_new = jnp.where(jn[None, :] < tq, i[:, None] - jn[None, :], -1).reshape(-1)
    new = expand(_bias_by_distance(rel_bias, d_new).reshape(h, tq, n_new))
    return far, last, new


def _oproj_kernel(a_ref, w_ref, x_ref, gate_ref, g_ref, b_ref, o_ref):
    out = jnp.dot(a_ref[...], w_ref[...], preferred_element_type=F32)
    r = DEEPNORM_ALPHA * x_ref[...] + gate_ref[...] * out
    o_ref[...] = _layer_norm(r, g_ref[...], b_ref[...])


def _oproj_call(a_bf, w_bf, x, mod, ln_g, ln_b):
    bx, tx, d = x.shape
    dk = a_bf.shape[-1]
    tm = min(512, tx)
    _, _, gate = _mod_specs(mod.shape[1], tm)
    tok = pl.BlockSpec((None, tm, d), lambda b, t: (b, t, 0))
    return pl.pallas_call(
        _oproj_kernel,
        grid=(bx, tx // tm),
        in_specs=[pl.BlockSpec((None, tm, dk), lambda b, t: (b, t, 0)), _const_spec(w_bf.shape), tok, gate,
                  _const_spec((1, d)), _const_spec((1, d))],
        out_specs=tok,
        out_shape=jax.ShapeDtypeStruct((bx, tx, d), F32),
        compiler_params=_cparams("arbitrary", "arbitrary"),
        name="attn_out_proj",
    )(a_bf, w_bf, x, mod, ln_g.reshape(1, d), ln_b.reshape(1, d))


def _ffn_kernel(x_ref, sh_ref, sc_ref, gate_ref, wi_ref, wo_ref, g_ref, b_ref, o_ref, acc_ref):
    x = x_ref[...]
    h = (x * (1.0 + sc_ref[...]) + sh_ref[...]).astype(BF16)
    for c in range(FFN_HIDDEN // FFN_CHUNK):
        lo = c * FFN_CHUNK
        gte = jnp.dot(h, wi_ref[:, lo:lo + FFN_CHUNK], preferred_element_type=F32)
        up = jnp.dot(h, wi_ref[:, FFN_HIDDEN + lo:FFN_HIDDEN + lo + FFN_CHUNK], preferred_element_type=F32)
        a = (_silu(gte) * up).astype(BF16)
        part = jnp.dot(a, wo_ref[lo:lo + FFN_CHUNK, :], preferred_element_type=F32)
        if c == 0:
            acc_ref[...] = part
        else:
            acc_ref[...] += part
    r = DEEPNORM_ALPHA * x + gate_ref[...] * acc_ref[...]
    o_ref[...] = _layer_norm(r, g_ref[...], b_ref[...])


def _ffn_call(x, mod, wi_bf, wo_bf, ln_g, ln_b):
    bx, tx, d = x.shape
    tm = min(512, tx)
    sh, sc, gate = _mod_specs(mod.shape[1], tm)
    tok = pl.BlockSpec((None, tm, d), lambda b, t: (b, t, 0))
    return pl.pallas_call(
        _ffn_kernel,
        grid=(bx, tx // tm),
        in_specs=[tok, sh, sc, gate, _const_spec(wi_bf.shape), _const_spec(wo_bf.shape),
                  _const_spec((1, d)), _const_spec((1, d))],
        out_specs=tok,
        out_shape=jax.ShapeDtypeStruct((bx, tx, d), F32),
        scratch_shapes=[pltpu.VMEM((tm, d), F32)],
        compiler_params=_cparams("arbitrary", "arbitrary"),
        name="ffn",
    )(x, mod, mod, mod, wi_bf, wo_bf, ln_g.reshape(1, d), ln_b.reshape(1, d))


def _ssd_in_kernel(x_ref, sh_ref, sc_ref, w_ref, z_ref, xbc_ref, dt_ref):
    h = (x_ref[...] * (1.0 + sc_ref[...]) + sh_ref[...]).astype(BF16)
    di = SSD_D_INNER
    for lo in range(0, di, 1024):
        z_ref[:, lo:lo + 1024] = jnp.dot(h, w_ref[:, lo:lo + 1024], preferred_element_type=F32)
    for lo in range(0, SSD_CONV_DIM, 1024):
        xbc_ref[:, lo:lo + 1024] = jnp.dot(h, w_ref[:, di + lo:di + lo + 1024], preferred_element_type=F32)
    lo = di + SSD_CONV_DIM
    dt_ref[...] = jnp.dot(h, w_ref[:, lo:lo + DT_PAD], preferred_element_type=F32)


def _ssd_in_call(x, mod, w_bf):
    bx, tx, d = x.shape
    tm = min(256, tx)
    sh, sc, _ = _mod_specs(mod.shape[1], tm)

    def tok(width):
        return pl.BlockSpec((None, tm, width), lambda b, t: (b, t, 0))

    return pl.pallas_call(
        _ssd_in_kernel,
        grid=(bx, tx // tm),
        in_specs=[tok(d), sh, sc, _const_spec(w_bf.shape)],
        out_specs=[tok(SSD_D_INNER), tok(SSD_CONV_DIM), tok(DT_PAD)],
        out_shape=[
            jax.ShapeDtypeStruct((bx, tx, SSD_D_INNER), F32),
            jax.ShapeDtypeStruct((bx, tx, SSD_CONV_DIM), F32),
            jax.ShapeDtypeStruct((bx, tx, DT_PAD), F32),
        ],
        compiler_params=_cparams("arbitrary", "arbitrary"),
        name="ssd_in_proj",
    )(x, mod, mod, w_bf)


def _ssd_scan_kernel(xbc_ref, dt_ref, conv0_ref, ssm0_ref, cw_ref, cb_ref, dtb_ref, alog_ref, dsk_ref,
                     y_ref, fin_ref, nconv_ref, ext_sc, xs_sc, st_sc, *, rows):
    L = SSD_CHUNK
    di = SSD_D_INNER
    c = pl.program_id(1)
    last = pl.num_programs(1) - 1
    tail = 8

    @pl.when(c == 0)
    def _():
        ext_sc[0:tail, :] = jnp.zeros((tail, SSD_CONV_DIM), F32)
        ext_sc[tail - (SSD_CONV - 1):tail, :] = conv0_ref[...]
        st_sc[...] = ssm0_ref[...]

    ext_sc[tail:tail + rows, :] = xbc_ref[...]
    if rows < L:
        ext_sc[tail + rows:tail + L, :] = jnp.zeros((L - rows, SSD_CONV_DIM), F32)

    @pl.when(c == last)
    def _():
        nconv_ref[...] = ext_sc[tail + rows - (SSD_CONV - 1):tail + rows, :]

    cblk = 512
    for lo in range(0, SSD_CONV_DIM, cblk):
        acc = cb_ref[:, lo:lo + cblk] + ext_sc[tail:tail + L, lo:lo + cblk] * cw_ref[3:4, lo:lo + cblk]
        for k in range(1, SSD_CONV):
            w_row = cw_ref[3 - k:4 - k, lo:lo + cblk]
            acc = acc + ext_sc[tail - k:tail - k + L, lo:lo + cblk] * w_row
        xs_sc[:, lo:lo + cblk] = _silu(acc)

    ext_sc[0:tail, :] = ext_sc[L:L + tail, :]

    if rows < L:
        dt_raw = jnp.concatenate([dt_ref[...], jnp.zeros((L - rows, DT_PAD), F32)], axis=0)
    else:
        dt_raw = dt_ref[...]
    xx = dt_raw + dtb_ref[...]
    dt = jnp.maximum(xx, 0.0) + jnp.log1p(jnp.exp(-jnp.abs(xx)))
    if rows < L:
        row = lax.broadcasted_iota(jnp.int32, (L, DT_PAD), 0)
        dt = jnp.where(row < rows, dt, 0.0)
    adt = -jnp.exp(alog_ref[...]) * dt
    ri = lax.broadcasted_iota(jnp.int32, (L, L), 0)
    ci = lax.broadcasted_iota(jnp.int32, (L, L), 1)
    causal = ci <= ri
    tril = jnp.where(causal, 1.0, 0.0).astype(F32)
    acs = jnp.dot(tril, adt, preferred_element_type=F32, precision=lax.Precision.HIGHEST)
    acs_t = acs.T
    dt_t = dt.T
    acs_end_t = acs_t[:, L - 1:L]
    w_t = dt_t * jnp.exp(acs_end_t - acs_t)
    cdec_t = jnp.exp(acs_end_t)

    half = lax.broadcasted_iota(jnp.int32, (L, LANES), 1) < SSD_HEADDIM
    for g in range(SSD_GROUPS):
        bm = xs_sc[:, di + g * SSD_STATE:di + (g + 1) * SSD_STATE]
        cm = xs_sc[:, di + SSD_GN + g * SSD_STATE:di + SSD_GN + (g + 1) * SSD_STATE]
        bm_bf = bm.astype(BF16)
        cb = _nt_dot(cm.astype(BF16), bm_bf)
        for jj in range(SSD_HEADS // SSD_GROUPS // 2):
            j = g * (SSD_HEADS // SSD_GROUPS // 2) + jj
            ps = slice(j * LANES, (j + 1) * LANES)
            x_pair = xs_sc[:, ps]
            x_bf = x_pair.astype(BF16)
            st_pair = st_sc[ps, :]
            st_bf = st_pair.astype(BF16)
            ys = []
            for hh in range(2):
                hd = 2 * j + hh
                acs_col = jnp.broadcast_to(acs[:, hd:hd + 1], (L, L))
                seg = acs_col - acs_t[hd:hd + 1, :]
                dec = jnp.exp(jnp.where(causal, seg, -jnp.inf))
                mh = (cb * dec * dt_t[hd:hd + 1, :]).astype(BF16)
                y_h = jnp.dot(mh, x_bf, preferred_element_type=F32)
                cme = (cm * jnp.exp(acs_col)).astype(BF16)
                y_h = y_h + _nt_dot(cme, st_bf)
                ys.append(y_h)
            y_pair = jnp.where(half, ys[0], ys[1]) + x_pair * dsk_ref[:, ps]
            y_ref[:, ps] = y_pair[0:rows]
            x_t = x_pair.T
            w_rows = jnp.concatenate([jnp.broadcast_to(w_t[2 * j:2 * j + 1, :], (SSD_HEADDIM, L)),
                                      jnp.broadcast_to(w_t[2 * j + 1:2 * j + 2, :], (SSD_HEADDIM, L))], axis=0)
            d_rows = jnp.concatenate([jnp.broadcast_to(cdec_t[2 * j:2 * j + 1, :], (SSD_HEADDIM, 1)),
                                      jnp.broadcast_to(cdec_t[2 * j + 1:2 * j + 2, :], (SSD_HEADDIM, 1))], axis=0)
            upd = jnp.dot((x_t * w_rows).astype(BF16), bm_bf, preferred_element_type=F32)
            st_sc[ps, :] = st_pair * d_rows + upd

    @pl.when(c == last)
    def _():
        fin_ref[...] = st_sc[...]


def _ssd_scan_call(xbc, dt, conv0, ssm0, conv_w, conv_b, dt_bias, a_log, d_skip):
    bs, tt, _ = xbc.shape
    rows = SSD_CHUNK if tt % SSD_CHUNK == 0 else tt
    nc = tt // rows
    heads_pad = DT_PAD - SSD_HEADS
    dtb = jnp.pad(dt_bias.astype(F32), (0, heads_pad)).reshape(1, DT_PAD)
    alog = jnp.pad(a_log.astype(F32), (0, heads_pad)).reshape(1, DT_PAD)
    dsk = jnp.repeat(d_skip.astype(F32), SSD_HEADDIM).reshape(1, SSD_D_INNER)

    def tok(width):
        return pl.BlockSpec((None, rows, width), lambda b, c: (b, c, 0))

    def per_seq(r, width):
        return pl.BlockSpec((None, r, width), lambda b, c: (b, 0, 0))

    return pl.pallas_call(
        functools.partial(_ssd_scan_kernel, rows=rows),
        grid=(bs, nc),
        in_specs=[tok(SSD_CONV_DIM), tok(DT_PAD), per_seq(SSD_CONV - 1, SSD_CONV_DIM),
                  per_seq(SSD_D_INNER, SSD_STATE),
                  _const_spec((SSD_CONV, SSD_CONV_DIM)), _const_spec((1, SSD_CONV_DIM)),
                  _const_spec((1, DT_PAD)), _const_spec((1, DT_PAD)), _const_spec((1, SSD_D_INNER))],
        out_specs=[tok(SSD_D_INNER), per_seq(SSD_D_INNER, SSD_STATE), per_seq(SSD_CONV - 1, SSD_CONV_DIM)],
        out_shape=[
            jax.ShapeDtypeStruct((bs, tt, SSD_D_INNER), F32),
            jax.ShapeDtypeStruct((bs, SSD_D_INNER, SSD_STATE), F32),
            jax.ShapeDtypeStruct((bs, SSD_CONV - 1, SSD_CONV_DIM), F32),
        ],
        scratch_shapes=[
            pltpu.VMEM((SSD_CHUNK + 8, SSD_CONV_DIM), F32),
            pltpu.VMEM((SSD_CHUNK, SSD_CONV_DIM), F32),
            pltpu.VMEM((SSD_D_INNER, SSD_STATE), F32),
        ],
        compiler_params=_cparams("arbitrary", "arbitrary"),
        name="ssd_scan",
    )(xbc, dt, conv0, ssm0, conv_w, conv_b.reshape(1, SSD_CONV_DIM), dtb, alog, dsk)


def _ssd_out_kernel(y_ref, z_ref, ng_ref, w_ref, x_ref, gate_ref, g_ref, b_ref, o_ref, a_sc):
    gw = SSD_D_INNER // SSD_GROUPS
    for g in range(SSD_GROUPS):
        gs = slice(g * gw, (g + 1) * gw)
        yz = y_ref[:, gs] * _silu(z_ref[:, gs])
        ms = jnp.mean(yz * yz, axis=-1, keepdims=True)
        a_sc[:, gs] = (yz * lax.rsqrt(ms + RMS_EPS) * ng_ref[:, gs]).astype(BF16)
    out = jnp.dot(a_sc[...], w_ref[...], preferred_element_type=F32)
    r = DEEPNORM_ALPHA * x_ref[...] + gate_ref[...] * out
    o_ref[...] = _layer_norm(r, g_ref[...], b_ref[...])


def _ssd_out_call(y, z, norm_g, w_bf, x, mod, ln_g, ln_b):
    bx, tx, d = x.shape
    tm = min(512, tx)
    _, _, gate = _mod_specs(mod.shape[1], tm)

    def tok(width):
        return pl.BlockSpec((None, tm, width), lambda b, t: (b, t, 0))

    return pl.pallas_call(
        _ssd_out_kernel,
        grid=(bx, tx // tm),
        in_specs=[tok(SSD_D_INNER), tok(SSD_D_INNER), _const_spec((1, SSD_D_INNER)), _const_spec(w_bf.shape),
                  tok(d), gate, _const_spec((1, d)), _const_spec((1, d))],
        out_specs=tok(d),
        out_shape=jax.ShapeDtypeStruct((bx, tx, d), F32),
        scratch_shapes=[pltpu.VMEM((tm, SSD_D_INNER), BF16)],
        compiler_params=_cparams("arbitrary", "arbitrary"),
        name="ssd_out_proj",
    )(y, z, norm_g.reshape(1, SSD_D_INNER), w_bf, x, mod, ln_g.reshape(1, d), ln_b.reshape(1, d))


def _attention_layer(j, lam_init, y_p, y_s, mod_p, mod_s, cache_k, cache_v, page_table, tables,
                     attn_w_qkv, attn_lambda, attn_subln_g, attn_w_o, ln_g, ln_b):
    b, t, d = y_p.shape
    bias_tiles, cfar, mask_far, mask_last, mask_new = tables
    w_qkv = attn_w_qkv[j].astype(BF16)
    w_o = attn_w_o[j].astype(BF16)
    lam_p = attn_lambda[j].astype(F32)
    g = attn_subln_g[j].astype(F32)

    q_bf, k_p, v_p, k_bf, v_bf = _qkv_call(y_p, mod_p, w_qkv)
    o_p = _attn_prompt_call(q_bf, k_bf, v_bf, bias_tiles, cfar, lam_p, g, lam_init)
    y_p = _oproj_call(o_p, w_o, y_p, mod_p, ln_g, ln_b)

    n_seq = page_table.shape[0]
    tq = y_s.shape[1] // n_seq
    n_pool = cache_k.shape[1]
    page_rows = PAGE_SIZE * ATT_HEADS
    pool_k = cache_k.reshape(cache_k.shape[0] * n_pool, page_rows, LANES)
    pool_v = cache_v.reshape(cache_v.shape[0] * n_pool, page_rows, LANES)
    q_bf, k_s, v_s, _, _ = _qkv_call(y_s, mod_s, w_qkv)
    new_rows = lambda a: a.reshape(n_seq, tq * ATT_HEADS, LANES)
    o_s = _attn_sample_call(q_bf.reshape(n_seq, tq, d), new_rows(k_s), new_rows(v_s), pool_k, pool_v,
                            page_table + j * n_pool, mask_far, mask_last, mask_new, lam_p, g, lam_init)
    y_s = _oproj_call(o_s.reshape(1, n_seq * tq, d), w_o, y_s, mod_s, ln_g, ln_b)

    kv = lambda a, n, tt: a.reshape(n, tt, ATT_HEADS, 2 * ATT_HD)
    return y_p, y_s, kv(k_p, b, t), kv(v_p, b, t), kv(k_s, n_seq, tq), kv(v_s, n_seq, tq)


def _ssd_layer(j, y_p, y_s, mod_p, mod_s, n_seq, state_ssm, state_conv, ssd_w_in, ssd_conv_w, ssd_conv_b,
               ssd_dt_bias, ssd_a_log, ssd_d, ssd_norm_g, ssd_w_out, ln_g, ln_b):
    b, t, d = y_p.shape
    tq = y_s.shape[1] // n_seq
    w_in = jnp.pad(ssd_w_in[j], ((0, 0), (0, DT_PAD - SSD_HEADS))).astype(BF16)
    w_out = ssd_w_out[j].astype(BF16)
    scan_w = (ssd_conv_w[j].astype(F32), ssd_conv_b[j].astype(F32), ssd_dt_bias[j], ssd_a_log[j], ssd_d[j])

    def mixer(y, mod, seqs, rows, conv0, ssm0):
        z, xbc, dt = _ssd_in_call(y, mod, w_in)
        per_seq = lambda a: a.reshape(seqs, rows, a.shape[-1])
        yy, fin, nconv = _ssd_scan_call(per_seq(xbc), per_seq(dt), conv0, ssm0, *scan_w)
        yy = yy.reshape(y.shape[0], y.shape[1], SSD_D_INNER)
        y_new = _ssd_out_call(yy, z, ssd_norm_g[j].astype(F32), w_out, y, mod, ln_g, ln_b)
        return y_new, fin.reshape(seqs, SSD_HEADS, SSD_HEADDIM, SSD_STATE), nconv

    zero_conv = jnp.zeros((b, SSD_CONV - 1, SSD_CONV_DIM), F32)
    zero_ssm = jnp.zeros((b, SSD_D_INNER, SSD_STATE), F32)
    y_p, ssm_p, conv_p = mixer(y_p, mod_p, b, t, zero_conv, zero_ssm)
    y_s, ssm_s, conv_s = mixer(y_s, mod_s, n_seq, tq, state_conv[j].astype(F32),
                               state_ssm[j].astype(F32).reshape(n_seq, SSD_D_INNER, SSD_STATE))
    return y_p, y_s, ssm_p, conv_p, ssm_s, conv_s


def kernel(x_prompt, x_sample, cache_k, cache_v, state_ssm, state_conv, page_table, c_prompt, c_sample,
           ada_w, ada_b, ln_g, ln_b, rel_bias, attn_w_qkv, attn_lambda, attn_subln_g, attn_w_o,
           ssd_w_in, ssd_conv_w, ssd_conv_b, ssd_dt_bias, ssd_a_log, ssd_d, ssd_norm_g, ssd_w_out,
           ffn_w_in, ffn_w_out):
    b, t, d = x_prompt.shape
    n_seq, tq, _ = x_sample.shape
    depth = ada_w.shape[0]

    n_c = b + n_seq
    rc = -(-n_c // 8) * 8
    c_all = jnp.concatenate([c_prompt, c_sample, jnp.zeros((rc - n_c, d), F32)], axis=0)
    mod_all = _mod_call(c_all, ada_w.reshape(depth * 2, d, 3 * d), ada_b.reshape(depth * 2, 3 * d))

    def mods(i, s):
        m = mod_all[2 * i + s]
        mod_p = m[0:b].reshape(b, 1, 3 * d)
        mod_s = jnp.repeat(m[b:b + n_seq], tq, axis=0).reshape(1, n_seq * tq, 3 * d)
        return mod_p, mod_s

    tile = min(ATTN_TILE, t)
    past = page_table.shape[1] * PAGE_SIZE
    bias_tiles = jnp.stack([_toeplitz_tile(rel_bias, tile, 0), _toeplitz_tile(rel_bias, tile, tile)], axis=1)
    cfar = _bias_by_distance(rel_bias, jnp.asarray([2 * tile], jnp.int32))[:, 0]
    tables = (bias_tiles, cfar) + _sample_masks(rel_bias, tq, past, PAGES_PER_STEP * PAGE_SIZE)

    y_p = x_prompt
    y_s = x_sample.reshape(1, n_seq * tq, d)
    k_p, v_p, k_s, v_s = [], [], [], []
    ssm_p, conv_p, ssm_s, conv_s = [], [], [], []
    for i in range(depth):
        j = i // 2
        mod_p, mod_s = mods(i, 0)
        if i % 2 == 0:
            lam_init = 0.8 - 0.6 * math.exp(-0.3 * i)
            y_p, y_s, kp, vp, ks, vs = _attention_layer(
                j, lam_init, y_p, y_s, mod_p, mod_s, cache_k, cache_v, page_table, tables,
                attn_w_qkv, attn_lambda, attn_subln_g, attn_w_o, ln_g[i, 0], ln_b[i, 0])
            k_p.append(kp)
            v_p.append(vp)
            k_s.append(ks)
            v_s.append(vs)
        else:
            y_p, y_s, sp, cp, ss, cs = _ssd_layer(
                j, y_p, y_s, mod_p, mod_s, n_seq, state_ssm, state_conv, ssd_w_in, ssd_conv_w, ssd_conv_b,
                ssd_dt_bias, ssd_a_log, ssd_d, ssd_norm_g, ssd_w_out, ln_g[i, 0], ln_b[i, 0])
            ssm_p.append(sp)
            conv_p.append(cp)
            ssm_s.append(ss)
            conv_s.append(cs)
        mod_p, mod_s = mods(i, 1)
        wi = ffn_w_in[i].astype(BF16)
        wo = ffn_w_out[i].astype(BF16)
        y_p = _ffn_call(y_p, mod_p, wi, wo, ln_g[i, 1], ln_b[i, 1])
        y_s = _ffn_call(y_s, mod_s, wi, wo, ln_g[i, 1], ln_b[i, 1])
    return (y_p, y_s.reshape(n_seq, tq, d), jnp.stack(k_p), jnp.stack(v_p), jnp.stack(k_s), jnp.stack(v_s),
            jnp.stack(ssm_p), jnp.stack(conv_p), jnp.stack(ssm_s), jnp.stack(conv_s))
```

```python
import functools
import math

import jax
import jax.numpy as jnp
from jax import lax
from jax.experimental import pallas as pl
from jax.experimental.pallas import tpu as pltpu

F32 = jnp.float32
BF16 = jnp.bfloat16

D_MODEL = 1024
DEPTH = 4
PAGE_SIZE = 128
ATT_HEADS = 8
ATT_HD = 64
ATT_VD = 128
REL_BUCKETS = 32
REL_MAX_DIST = 128
SSD_D_INNER = 2048
SSD_HEADDIM = 64
SSD_HEADS = 32
SSD_GROUPS = 4
SSD_STATE = 128
SSD_CONV = 4
SSD_GN = SSD_GROUPS * SSD_STATE
SSD_CONV_DIM = SSD_D_INNER + 2 * SSD_GN
SSD_CHUNK = 128
FFN_HIDDEN = 2816
DEEPNORM_ALPHA = (2 * DEPTH) ** 0.25
LN_EPS = 1e-5
RMS_EPS = 1e-5

LANES = 128
DT_PAD = LANES
NEG_BIG = -1e30
LOG2E = math.log2(math.e)
VMEM_LIMIT = 56 * 1024 * 1024
ATTN_TILE = 512
ATTN_ROWS = 128
FAR_TILES_PER_STEP = 4
ATTN_HEADS_PER_STEP = 2
PAGES_PER_STEP = 8
FFN_CHUNK = 256


def _cparams(*sem):
    return pltpu.CompilerParams(dimension_semantics=sem, vmem_limit_bytes=VMEM_LIMIT)


def _silu(x):
    return x * jax.nn.sigmoid(x)


def _layer_norm(r, g, b):
    mu = jnp.mean(r, axis=-1, keepdims=True)
    d = r - mu
    var = jnp.mean(d * d, axis=-1, keepdims=True)
    return d * lax.rsqrt(var + LN_EPS) * g + b


def _mod_specs(mod_rows, tm):
    rows = 1 if mod_rows == 1 else tm

    def spec(piece):
        if mod_rows == 1:
            return pl.BlockSpec((None, 1, D_MODEL), lambda b, t: (b, 0, piece))
        return pl.BlockSpec((None, rows, D_MODEL), lambda b, t: (b, t, piece))

    return spec(0), spec(1), spec(2)


def _const_spec(shape):
    n = len(shape)
    return pl.BlockSpec(shape, lambda b, t: (0,) * n)


def _mod_kernel(c_ref, w_ref, b_ref, o_ref):
    a = _silu(c_ref[...]).astype(BF16)
    o_ref[...] = jnp.dot(a, w_ref[...].astype(BF16), preferred_element_type=F32) + b_ref[...]


def _mod_call(c_all, ada_w, ada_b):
    n_sub, d, d3 = ada_w.shape
    rc = c_all.shape[0]
    tn = 1024
    return pl.pallas_call(
        _mod_kernel,
        grid=(n_sub, d3 // tn),
        in_specs=[
            pl.BlockSpec((rc, d), lambda i, j: (0, 0)),
            pl.BlockSpec((None, d, tn), lambda i, j: (i, 0, j)),
            pl.BlockSpec((None, 1, tn), lambda i, j: (i, 0, j)),
        ],
        out_specs=pl.BlockSpec((None, rc, tn), lambda i, j: (i, 0, j)),
        out_shape=jax.ShapeDtypeStruct((n_sub, rc, d3), F32),
        compiler_params=_cparams("arbitrary", "arbitrary"),
        name="adaln_mod",
    )(c_all, ada_w, ada_b.reshape(n_sub, 1, d3))


def _qkv_kernel(x_ref, sh_ref, sc_ref, w_ref, q_ref, k_ref, v_ref, kb_ref, vb_ref):
    d = D_MODEL
    h = (x_ref[...] * (1.0 + sc_ref[...]) + sh_ref[...]).astype(BF16)
    q = jnp.dot(h, w_ref[:, 0:d], preferred_element_type=F32)
    q_ref[...] = (q * (ATT_HD ** -0.5 * LOG2E)).astype(BF16)
    tm = x_ref.shape[0]
    k = jnp.dot(h, w_ref[:, d:2 * d], preferred_element_type=F32)
    kb_ref[...] = k.astype(BF16)
    v = jnp.dot(h, w_ref[:, 2 * d:3 * d], preferred_element_type=F32)
    vb_ref[...] = v.astype(BF16)
    for hd in range(ATT_HEADS):
        hs = slice(hd * LANES, (hd + 1) * LANES)
        k_ref[pl.ds(hd, tm, stride=ATT_HEADS), :] = k[:, hs]
        v_ref[pl.ds(hd, tm, stride=ATT_HEADS), :] = v[:, hs]


def _qkv_call(x, mod, w_bf):
    bx, tx, d = x.shape
    tm = min(512, tx)
    sh, sc, _ = _mod_specs(mod.shape[1], tm)
    tok = pl.BlockSpec((None, tm, d), lambda b, t: (b, t, 0))
    rows = pl.BlockSpec((None, tm * ATT_HEADS, LANES), lambda b, t: (b, t, 0))
    return pl.pallas_call(
        _qkv_kernel,
        grid=(bx, tx // tm),
        in_specs=[tok, sh, sc, _const_spec(w_bf.shape)],
        out_specs=[tok, rows, rows, tok, tok],
        out_shape=[
            jax.ShapeDtypeStruct((bx, tx, d), BF16),
            jax.ShapeDtypeStruct((bx, tx * ATT_HEADS, LANES), F32),
            jax.ShapeDtypeStruct((bx, tx * ATT_HEADS, LANES), F32),
            jax.ShapeDtypeStruct((bx, tx, d), BF16),
            jax.ShapeDtypeStruct((bx, tx, d), BF16),
        ],
        compiler_params=_cparams("arbitrary", "arbitrary"),
        name="qkv_proj",
    )(x, mod, mod, w_bf)


def _stack_maps(q):
    lane = lax.broadcasted_iota(jnp.int32, q.shape, 1)
    zero = jnp.zeros_like(q)
    return jnp.concatenate([jnp.where(lane < ATT_HD, q, zero), jnp.where(lane >= ATT_HD, q, zero)], axis=0)


def _lambda_full(lam_ref, lam_init):
    lp = lam_ref[...]
    a = jnp.sum(lp[0:1, :] * lp[1:2, :], axis=-1, keepdims=True)
    b = jnp.sum(lp[2:3, :] * lp[3:4, :], axis=-1, keepdims=True)
    return jnp.exp(a) - jnp.exp(b) + lam_init


def _diff_finish(num, den, t, lam, g, lam_init):
    o = num[0:t] / den[0:t] - lam * (num[t:2 * t] / den[t:2 * t])
    ms = jnp.mean(o * o, axis=-1, keepdims=True)
    return o * lax.rsqrt(ms + RMS_EPS) * g * (1.0 - lam_init)


def _nt_dot(a, b):
    return lax.dot_general(a, b, (((1,), (1,)), ((), ())), preferred_element_type=F32)


def _lane_tile(x, n):
    return jnp.concatenate([x] * n, axis=1)


def _online_softmax(s, shift_bias, v_ext, m_prev, acc_prev):
    m_curr = jnp.max(s, axis=1, keepdims=True)
    if shift_bias is not None:
        m_curr = m_curr + shift_bias
    m_next = jnp.maximum(m_prev, m_curr)
    shift = m_next if shift_bias is None else m_next - shift_bias
    p = jnp.exp2(s - _lane_tile(shift, s.shape[1] // LANES))
    alpha = jnp.exp2(m_prev - m_next)
    acc = _lane_tile(alpha, 2) * acc_prev + jnp.dot(p.astype(BF16), v_ext, preferred_element_type=F32)
    return m_next, acc


def _attn_prompt_kernel(q_ref, k_ref, v_ref, bias_ref, cfar_ref, lam_ref, g_ref, o_ref,
                        qq_sc, vext_sc, m_sc, acc_sc, *, t, lam_init):
    hp = ATTN_HEADS_PER_STEP
    head0 = pl.program_id(1) * hp
    qi = pl.program_id(2)
    rs = min(ATTN_ROWS, t)

    @pl.when(qi == 0)
    def _():
        for hh in range(hp):
            vext_sc[hh, :, 0:ATT_VD] = v_ref[:, hh * LANES:(hh + 1) * LANES]
            vext_sc[hh, :, ATT_VD:2 * ATT_VD] = jnp.ones((vext_sc.shape[1], ATT_VD), BF16)

    for hh in range(hp):
        qq_sc[hh] = _stack_maps(q_ref[:, hh * LANES:(hh + 1) * LANES])
    m_sc[...] = jnp.full(m_sc.shape, NEG_BIG, F32)
    acc_sc[...] = jnp.zeros(acc_sc.shape, F32)

    def tile_update(ki, width, bias_col=None):
        start = pl.multiple_of(ki * t, t)
        for hh in range(hp):
            hs = slice(hh * LANES, (hh + 1) * LANES)
            for r in range(2 * t // rs):
                rows = slice(r * rs, (r + 1) * rs)
                b0 = (r * rs) % t
                w = width if bias_col is None else width - t + b0 + rs
                s = _nt_dot(qq_sc[hh, rows, :], k_ref[pl.ds(start, w), hs])
                v_ext = vext_sc[hh, pl.ds(start, w), :]
                if bias_col is None:
                    m, acc = _online_softmax(s, cfar_ref[head0 + hh], v_ext, m_sc[hh, rows, :], acc_sc[hh, rows, :])
                else:
                    s = s + bias_ref[hh, b0:b0 + rs, bias_col:bias_col + w]
                    m, acc = _online_softmax(s, None, v_ext, m_sc[hh, rows, :], acc_sc[hh, rows, :])
                m_sc[hh, rows, :] = m
                acc_sc[hh, rows, :] = acc

    n_far = jnp.maximum(qi - 1, 0)
    n_wide = n_far // FAR_TILES_PER_STEP

    def wide_body(i, carry):
        tile_update(i * FAR_TILES_PER_STEP, FAR_TILES_PER_STEP * t)
        return carry

    def far_body(ki, carry):
        tile_update(ki, t)
        return carry

    lax.fori_loop(0, n_wide, wide_body, 0)
    lax.fori_loop(n_wide * FAR_TILES_PER_STEP, n_far, far_body, 0)

    @pl.when(qi >= 1)
    def _():
        tile_update(qi - 1, 2 * t, 0)

    @pl.when(qi == 0)
    def _():
        tile_update(0, t, t)

    lam = _lambda_full(lam_ref, lam_init)
    for hh in range(hp):
        acc = acc_sc[hh]
        o_ref[:, hh * LANES:(hh + 1) * LANES] = _diff_finish(
            acc[:, 0:ATT_VD], acc[:, ATT_VD:2 * ATT_VD], t, lam, g_ref[...], lam_init).astype(BF16)


def _attn_prompt_call(q_bf, k_bf, v_bf, bias_tiles, cfar, lam_p, g, lam_init):
    b, tt, d = q_bf.shape
    t = bias_tiles.shape[1]
    hp = ATTN_HEADS_PER_STEP
    qtile = pl.BlockSpec((None, t, hp * LANES), lambda b, h, q: (b, q, h))
    heads = pl.BlockSpec((None, tt, hp * LANES), lambda b, h, q: (b, 0, h))
    return pl.pallas_call(
        functools.partial(_attn_prompt_kernel, t=t, lam_init=lam_init),
        grid=(b, ATT_HEADS // hp, tt // t),
        in_specs=[
            qtile, heads, heads,
            pl.BlockSpec((hp, t, 2 * t), lambda b, h, q: (h, 0, 0)),
            pl.BlockSpec(memory_space=pltpu.SMEM),
            pl.BlockSpec((4, ATT_HD), lambda b, h, q: (0, 0)),
            pl.BlockSpec((1, ATT_VD), lambda b, h, q: (0, 0)),
        ],
        out_specs=qtile,
        out_shape=jax.ShapeDtypeStruct((b, tt, d), BF16),
        scratch_shapes=[
            pltpu.VMEM((hp, 2 * t, LANES), BF16),
            pltpu.VMEM((hp, tt, 2 * ATT_VD), BF16),
            pltpu.VMEM((hp, 2 * t, LANES), F32),
            pltpu.VMEM((hp, 2 * t, 2 * ATT_VD), F32),
        ],
        compiler_params=_cparams("arbitrary", "arbitrary", "arbitrary"),
        name="attn_prompt",
    )(q_bf, k_bf, v_bf, bias_tiles, cfar, lam_p, g.reshape(1, ATT_VD))


def _attn_sample_kernel(pt_ref, q_ref, kn_ref, vn_ref, *rest, tq, n_pages, lam_init):
    k_pages = rest[:n_pages]
    v_pages = rest[n_pages:2 * n_pages]
    (mask_far_ref, mask_last_ref, mask_new_ref, lam_ref, g_ref, o_ref,
     qq_sc, k_sc, v_sc, m_sc, l_sc, acc_sc) = rest[2 * n_pages:]
    c = pl.program_id(1)
    last = pl.num_programs(1) - 1
    page_rows = PAGE_SIZE * ATT_HEADS

    @pl.when(c == 0)
    def _():
        q = q_ref[...].astype(F32)
        for h in range(ATT_HEADS):
            qq_sc[h * 2 * tq:(h + 1) * 2 * tq, :] = _stack_maps(q[:, h * LANES:(h + 1) * LANES]).astype(BF16)
        m_sc[...] = jnp.full(m_sc.shape, NEG_BIG, F32)
        l_sc[...] = jnp.zeros(l_sc.shape, F32)
        acc_sc[...] = jnp.zeros(acc_sc.shape, F32)

    for j in range(n_pages):
        k_sc[j * page_rows:(j + 1) * page_rows, :] = k_pages[j][...].astype(BF16)
        v_sc[j * page_rows:(j + 1) * page_rows, :] = v_pages[j][...].astype(BF16)

    def update(s, v):
        m_prev = m_sc[...]
        m_next = jnp.maximum(m_prev, jnp.max(s, axis=1, keepdims=True))
        p = jnp.exp2(s - m_next)
        alpha = jnp.exp2(m_prev - m_next)
        l_sc[...] = alpha * l_sc[...] + jnp.sum(p, axis=1, keepdims=True)
        acc_sc[...] = alpha * acc_sc[...] + jnp.dot(p.astype(BF16), v, preferred_element_type=F32)
        m_sc[...] = m_next

    def past_logits():
        s = _nt_dot(qq_sc[...], k_sc[...])
        return s

    @pl.when(c < last)
    def _():
        update(past_logits() + _lane_tile(mask_far_ref[...], n_pages), v_sc[...])

    @pl.when(c == last)
    def _():
        update(past_logits() + mask_last_ref[...], v_sc[...])
        rows_new = kn_ref.shape[0]
        pad = jnp.zeros((LANES - rows_new, LANES), F32)
        kn = jnp.concatenate([kn_ref[...], pad], axis=0).astype(BF16)
        vn = jnp.concatenate([vn_ref[...], pad], axis=0).astype(BF16)
        update(_nt_dot(qq_sc[...], kn) + mask_new_ref[...], vn)
        lam = _lambda_full(lam_ref, lam_init)
        num = acc_sc[...]
        den = l_sc[...]
        for h in range(ATT_HEADS):
            rows = slice(h * 2 * tq, (h + 1) * 2 * tq)
            o_ref[:, h * LANES:(h + 1) * LANES] = _diff_finish(
                num[rows], den[rows], tq, lam, g_ref[...], lam_init).astype(BF16)


def _attn_sample_call(q_bf, k_new, v_new, pool_k, pool_v, pages, mask_far, mask_last, mask_new,
                      lam_p, g, lam_init):
    bs, tq, d = q_bf.shape
    n_pages_total = pages.shape[1]
    pp = PAGES_PER_STEP
    n_chunks = n_pages_total // pp
    page_rows = PAGE_SIZE * ATT_HEADS
    n_rows = ATT_HEADS * 2 * tq
    tok = pl.BlockSpec((None, tq, d), lambda b, c, pt: (b, 0, 0))
    new = pl.BlockSpec((None, tq * ATT_HEADS, LANES), lambda b, c, pt: (b, 0, 0))

    def page_spec(j):
        return pl.BlockSpec((None, page_rows, LANES), lambda b, c, pt: (pt[b, c * pp + j], 0, 0))

    def const(shape):
        n = len(shape)
        return pl.BlockSpec(shape, lambda b, c, pt: (0,) * n)

    grid_spec = pltpu.PrefetchScalarGridSpec(
        num_scalar_prefetch=1,
        grid=(bs, n_chunks),
        in_specs=[tok, new, new]
        + [page_spec(j) for j in range(pp)]
        + [page_spec(j) for j in range(pp)]
        + [const(mask_far.shape), const(mask_last.shape), const(mask_new.shape),
           const((4, ATT_HD)), const((1, ATT_VD))],
        out_specs=tok,
        scratch_shapes=[
            pltpu.VMEM((n_rows, LANES), BF16),
            pltpu.VMEM((pp * page_rows, LANES), BF16),
            pltpu.VMEM((pp * page_rows, LANES), BF16),
            pltpu.VMEM((n_rows, 1), F32),
            pltpu.VMEM((n_rows, 1), F32),
            pltpu.VMEM((n_rows, ATT_VD), F32),
        ],
    )
    return pl.pallas_call(
        functools.partial(_attn_sample_kernel, tq=tq, n_pages=pp, lam_init=lam_init),
        grid_spec=grid_spec,
        out_shape=jax.ShapeDtypeStruct((bs, tq, d), BF16),
        compiler_params=_cparams("arbitrary", "arbitrary"),
        name="attn_sample",
    )(pages, q_bf, k_new, v_new, *([pool_k] * pp), *([pool_v] * pp), mask_far, mask_last, mask_new,
      lam_p, g.reshape(1, ATT_VD))


def _bias_by_distance(rel_bias, n):
    max_exact = REL_BUCKETS // 2
    nn = jnp.maximum(n, 0)
    nf = jnp.maximum(nn, 1).astype(F32)
    large = max_exact + (jnp.log(nf / max_exact) / math.log(REL_MAX_DIST / max_exact)
                         * (REL_BUCKETS - max_exact)).astype(jnp.int32)
    bucket = jnp.where(nn < max_exact, nn, jnp.minimum(large, REL_BUCKETS - 1))
    table = (rel_bias.astype(F32) * LOG2E).T
    onehot = (bucket[None, :] == jnp.arange(REL_BUCKETS)[:, None]).astype(F32)
    vals = jnp.dot(table, onehot, precision=lax.Precision.HIGHEST)
    return jnp.where((n >= 0)[None, :], vals, NEG_BIG)


def _toeplitz_tile(rel_bias, t, offset):
    m = 2 * t
    k = jnp.arange(m)
    dist = jnp.where(k < t, offset - k, offset + m - k)
    v = _bias_by_distance(rel_bias, dist)
    flat = jnp.tile(v, (1, t))[:, :t * (m - 1)]
    return flat.reshape(-1, t, m - 1)[:, :, :t]


def _sample_masks(rel_bias, tq, past, span):
    h = ATT_HEADS
    same = (jnp.arange(h)[:, None] == jnp.arange(h)[None, :])
    cfar = _bias_by_distance(rel_bias, jnp.asarray([past], jnp.int32))[:, 0]

    def expand(base):
        n_tok = base.shape[-1]
        full = jnp.where(same[:, None, None, None, :], base[:, None, :, :, None], NEG_BIG)
        full = jnp.broadcast_to(full, (h, 2, tq, n_tok, h))
        return full.reshape(h * 2 * tq, n_tok * h)

    i = jnp.arange(tq)
    far = expand(jnp.broadcast_to(cfar[:, None, None], (h, tq, PAGE_SIZE)))
    j = jnp.arange(span)
    d_last = (span + i[:, None] - j[None, :]).reshape(-1)
    last = expand(_bias_by_distance(rel_bias, d_last).reshape(h, tq, span))
    n_new = LANES // h
    jn = jnp.arange(n_new)
    d_new = jnp.where(jn[None, :] < tq, i[:, None] - jn[None, :], -1).reshape(-1)
    new = expand(_bias_by_distance(rel_bias, d_new).reshape(h, tq, n_new))
    return far, last, new


def _oproj_kernel(a_ref, w_ref, x_ref, gate_ref, g_ref, b_ref, o_ref):
    out = jnp.dot(a_ref[...], w_ref[...], preferred_element_type=F32)
    r = DEEPNORM_ALPHA * x_ref[...] + gate_ref[...] * out
    o_ref[...] = _layer_norm(r, g_ref[...], b_ref[...])


def _oproj_call(a_bf, w_bf, x, mod, ln_g, ln_b):
    bx, tx, d = x.shape
    dk = a_bf.shape[-1]
    tm = min(512, tx)
    _, _, gate = _mod_specs(mod.shape[1], tm)
    tok = pl.BlockSpec((None, tm, d), lambda b, t: (b, t, 0))
    return pl.pallas_call(
        _oproj_kernel,
        grid=(bx, tx // tm),
        in_specs=[pl.BlockSpec((None, tm, dk), lambda b, t: (b, t, 0)), _const_spec(w_bf.shape), tok, gate,
                  _const_spec((1, d)), _const_spec((1, d))],
        out_specs=tok,
        out_shape=jax.ShapeDtypeStruct((bx, tx, d), F32),
        compiler_params=_cparams("arbitrary", "arbitrary"),
        name="attn_out_proj",
    )(a_bf, w_bf, x, mod, ln_g.reshape(1, d), ln_b.reshape(1, d))


def _ffn_kernel(x_ref, sh_ref, sc_ref, gate_ref, wi_ref, wo_ref, g_ref, b_ref, o_ref, acc_ref):
    x = x_ref[...]
    h = (x * (1.0 + sc_ref[...]) + sh_ref[...]).astype(BF16)
    for c in range(FFN_HIDDEN // FFN_CHUNK):
        lo = c * FFN_CHUNK
        gte = jnp.dot(h, wi_ref[:, lo:lo + FFN_CHUNK], preferred_element_type=F32)
        up = jnp.dot(h, wi_ref[:, FFN_HIDDEN + lo:FFN_HIDDEN + lo + FFN_CHUNK], preferred_element_type=F32)
        a = (_silu(gte) * up).astype(BF16)
        part = jnp.dot(a, wo_ref[lo:lo + FFN_CHUNK, :], preferred_element_type=F32)
        if c == 0:
            acc_ref[...] = part
        else:
            acc_ref[...] += part
    r = DEEPNORM_ALPHA * x + gate_ref[...] * acc_ref[...]
    o_ref[...] = _layer_norm(r, g_ref[...], b_ref[...])


def _ffn_call(x, mod, wi_bf, wo_bf, ln_g, ln_b):
    bx, tx, d = x.shape
    tm = min(512, tx)
    sh, sc, gate = _mod_specs(mod.shape[1], tm)
    tok = pl.BlockSpec((None, tm, d), lambda b, t: (b, t, 0))
    return pl.pallas_call(
        _ffn_kernel,
        grid=(bx, tx // tm),
        in_specs=[tok, sh, sc, gate, _const_spec(wi_bf.shape), _const_spec(wo_bf.shape),
                  _const_spec((1, d)), _const_spec((1, d))],
        out_specs=tok,
        out_shape=jax.ShapeDtypeStruct((bx, tx, d), F32),
        scratch_shapes=[pltpu.VMEM((tm, d), F32)],
        compiler_params=_cparams("arbitrary", "arbitrary"),
        name="ffn",
    )(x, mod, mod, mod, wi_bf, wo_bf, ln_g.reshape(1, d), ln_b.reshape(1, d))


def _ssd_in_kernel(x_ref, sh_ref, sc_ref, w_ref, z_ref, xbc_ref, dt_ref):
    h = (x_ref[...] * (1.0 + sc_ref[...]) + sh_ref[...]).astype(BF16)
    di = SSD_D_INNER
    for lo in range(0, di, 1024):
        z_ref[:, lo:lo + 1024] = jnp.dot(h, w_ref[:, lo:lo + 1024], preferred_element_type=F32)
    for lo in range(0, SSD_CONV_DIM, 1024):
        xbc_ref[:, lo:lo + 1024] = jnp.dot(h, w_ref[:, di + lo:di + lo + 1024], preferred_element_type=F32)
    lo = di + SSD_CONV_DIM
    dt_ref[...] = jnp.dot(h, w_ref[:, lo:lo + DT_PAD], preferred_element_type=F32)


def _ssd_in_call(x, mod, w_bf):
    bx, tx, d = x.shape
    tm = min(256, tx)
    sh, sc, _ = _mod_specs(mod.shape[1], tm)

    def tok(width):
        return pl.BlockSpec((None, tm, width), lambda b, t: (b, t, 0))

    return pl.pallas_call(
        _ssd_in_kernel,
        grid=(bx, tx // tm),
        in_specs=[tok(d), sh, sc, _const_spec(w_bf.shape)],
        out_specs=[tok(SSD_D_INNER), tok(SSD_CONV_DIM), tok(DT_PAD)],
        out_shape=[
            jax.ShapeDtypeStruct((bx, tx, SSD_D_INNER), F32),
            jax.ShapeDtypeStruct((bx, tx, SSD_CONV_DIM), F32),
            jax.ShapeDtypeStruct((bx, tx, DT_PAD), F32),
        ],
        compiler_params=_cparams("arbitrary", "arbitrary"),
        name="ssd_in_proj",
    )(x, mod, mod, w_bf)


def _ssd_scan_kernel(xbc_ref, dt_ref, conv0_ref, ssm0_ref, cw_ref, cb_ref, dtb_ref, alog_ref, dsk_ref,
                     y_ref, fin_ref, nconv_ref, ext_sc, xs_sc, st_sc, *, rows):
    L = SSD_CHUNK
    di = SSD_D_INNER
    c = pl.program_id(1)
    last = pl.num_programs(1) - 1
    tail = 8

    @pl.when(c == 0)
    def _():
        ext_sc[0:tail, :] = jnp.zeros((tail, SSD_CONV_DIM), F32)
        ext_sc[tail - (SSD_CONV - 1):tail, :] = conv0_ref[...]
        st_sc[...] = ssm0_ref[...]

    ext_sc[tail:tail + rows, :] = xbc_ref[...]
    if rows < L:
        ext_sc[tail + rows:tail + L, :] = jnp.zeros((L - rows, SSD_CONV_DIM), F32)

    @pl.when(c == last)
    def _():
        nconv_ref[...] = ext_sc[tail + rows - (SSD_CONV - 1):tail + rows, :]

    cblk = 512
    for lo in range(0, SSD_CONV_DIM, cblk):
        acc = cb_ref[:, lo:lo + cblk] + ext_sc[tail:tail + L, lo:lo + cblk] * cw_ref[3:4, lo:lo + cblk]
        for k in range(1, SSD_CONV):
            w_row = cw_ref[3 - k:4 - k, lo:lo + cblk]
            acc = acc + ext_sc[tail - k:tail - k + L, lo:lo + cblk] * w_row
        xs_sc[:, lo:lo + cblk] = _silu(acc)

    ext_sc[0:tail, :] = ext_sc[L:L + tail, :]

    if rows < L:
        dt_raw = jnp.concatenate([dt_ref[...], jnp.zeros((L - rows, DT_PAD), F32)], axis=0)
    else:
        dt_raw = dt_ref[...]
    xx = dt_raw + dtb_ref[...]
    dt = jnp.maximum(xx, 0.0) + jnp.log1p(jnp.exp(-jnp.abs(xx)))
    if rows < L:
        row = lax.broadcasted_iota(jnp.int32, (L, DT_PAD), 0)
        dt = jnp.where(row < rows, dt, 0.0)
    adt = -jnp.exp(alog_ref[...]) * dt
    ri = lax.broadcasted_iota(jnp.int32, (L, L), 0)
    ci = lax.broadcasted_iota(jnp.int32, (L, L), 1)
    causal = ci <= ri
    tril = jnp.where(causal, 1.0, 0.0).astype(F32)
    acs = jnp.dot(tril, adt, preferred_element_type=F32, precision=lax.Precision.HIGHEST)
    acs_t = acs.T
    dt_t = dt.T
    acs_end_t = acs_t[:, L - 1:L]
    w_t = dt_t * jnp.exp(acs_end_t - acs_t)
    cdec_t = jnp.exp(acs_end_t)

    half = lax.broadcasted_iota(jnp.int32, (L, LANES), 1) < SSD_HEADDIM
    for g in range(SSD_GROUPS):
        bm = xs_sc[:, di + g * SSD_STATE:di + (g + 1) * SSD_STATE]
        cm = xs_sc[:, di + SSD_GN + g * SSD_STATE:di + SSD_GN + (g + 1) * SSD_STATE]
        bm_bf = bm.astype(BF16)
        cb = _nt_dot(cm.astype(BF16), bm_bf)
        for jj in range(SSD_HEADS // SSD_GROUPS // 2):
            j = g * (SSD_HEADS // SSD_GROUPS // 2) + jj
            ps = slice(j * LANES, (j + 1) * LANES)
            x_pair = xs_sc[:, ps]
            x_bf = x_pair.astype(BF16)
            st_pair = st_sc[ps, :]
            st_bf = st_pair.astype(BF16)
            ys = []
            for hh in range(2):
                hd = 2 * j + hh
                acs_col = jnp.broadcast_to(acs[:, hd:hd + 1], (L, L))
                seg = acs_col - acs_t[hd:hd + 1, :]
                dec = jnp.exp(jnp.where(causal, seg, -jnp.inf))
                mh = (cb * dec * dt_t[hd:hd + 1, :]).astype(BF16)
                y_h = jnp.dot(mh, x_bf, preferred_element_type=F32)
                cme = (cm * jnp.exp(acs_col)).astype(BF16)
                y_h = y_h + _nt_dot(cme, st_bf)
                ys.append(y_h)
            y_pair = jnp.where(half, ys[0], ys[1]) + x_pair * dsk_ref[:, ps]
            y_ref[:, ps] = y_pair[0:rows]
            x_t = x_pair.T
            w_rows = jnp.concatenate([jnp.broadcast_to(w_t[2 * j:2 * j + 1, :], (SSD_HEADDIM, L)),
                                      jnp.broadcast_to(w_t[2 * j + 1:2 * j + 2, :], (SSD_HEADDIM, L))], axis=0)
            d_rows = jnp.concatenate([jnp.broadcast_to(cdec_t[2 * j:2 * j + 1, :], (SSD_HEADDIM, 1)),
                                      jnp.broadcast_to(cdec_t[2 * j + 1:2 * j + 2, :], (SSD_HEADDIM, 1))], axis=0)
            upd = jnp.dot((x_t * w_rows).astype(BF16), bm_bf, preferred_element_type=F32)
            st_sc[ps, :] = st_pair * d_rows + upd

    @pl.when(c == last)
    def _():
        fin_ref[...] = st_sc[...]


def _ssd_scan_call(xbc, dt, conv0, ssm0, conv_w, conv_b, dt_bias, a_log, d_skip):
    bs, tt, _ = xbc.shape
    rows = SSD_CHUNK if tt % SSD_CHUNK == 0 else tt
    nc = tt // rows
    heads_pad = DT_PAD - SSD_HEADS
    dtb = jnp.pad(dt_bias.astype(F32), (0, heads_pad)).reshape(1, DT_PAD)
    alog = jnp.pad(a_log.astype(F32), (0, heads_pad)).reshape(1, DT_PAD)
    dsk = jnp.repeat(d_skip.astype(F32), SSD_HEADDIM).reshape(1, SSD_D_INNER)

    def tok(width):
        return pl.BlockSpec((None, rows, width), lambda b, c: (b, c, 0))

    def per_seq(r, width):
        return pl.BlockSpec((None, r, width), lambda b, c: (b, 0, 0))

    return pl.pallas_call(
        functools.partial(_ssd_scan_kernel, rows=rows),
        grid=(bs, nc),
        in_specs=[tok(SSD_CONV_DIM), tok(DT_PAD), per_seq(SSD_CONV - 1, SSD_CONV_DIM),
                  per_seq(SSD_D_INNER, SSD_STATE),
                  _const_spec((SSD_CONV, SSD_CONV_DIM)), _const_spec((1, SSD_CONV_DIM)),
                  _const_spec((1, DT_PAD)), _const_spec((1, DT_PAD)), _const_spec((1, SSD_D_INNER))],
        out_specs=[tok(SSD_D_INNER), per_seq(SSD_D_INNER, SSD_STATE), per_seq(SSD_CONV - 1, SSD_CONV_DIM)],
        out_shape=[
            jax.ShapeDtypeStruct((bs, tt, SSD_D_INNER), F32),
            jax.ShapeDtypeStruct((bs, SSD_D_INNER, SSD_STATE), F32),
            jax.ShapeDtypeStruct((bs, SSD_CONV - 1, SSD_CONV_DIM), F32),
        ],
        scratch_shapes=[
            pltpu.VMEM((SSD_CHUNK + 8, SSD_CONV_DIM), F32),
            pltpu.VMEM((SSD_CHUNK, SSD_CONV_DIM), F32),
            pltpu.VMEM((SSD_D_INNER, SSD_STATE), F32),
        ],
        compiler_params=_cparams("arbitrary", "arbitrary"),
        name="ssd_scan",
    )(xbc, dt, conv0, ssm0, conv_w, conv_b.reshape(1, SSD_CONV_DIM), dtb, alog, dsk)


def _ssd_out_kernel(y_ref, z_ref, ng_ref, w_ref, x_ref, gate_ref, g_ref, b_ref, o_ref, a_sc):
    gw = SSD_D_INNER // SSD_GROUPS
    for g in range(SSD_GROUPS):
        gs = slice(g * gw, (g + 1) * gw)
        yz = y_ref[:, gs] * _silu(z_ref[:, gs])
        ms = jnp.mean(yz * yz, axis=-1, keepdims=True)
        a_sc[:, gs] = (yz * lax.rsqrt(ms + RMS_EPS) * ng_ref[:, gs]).astype(BF16)
    out = jnp.dot(a_sc[...], w_ref[...], preferred_element_type=F32)
    r = DEEPNORM_ALPHA * x_ref[...] + gate_ref[...] * out
    o_ref[...] = _layer_norm(r, g_ref[...], b_ref[...])


def _ssd_out_call(y, z, norm_g, w_bf, x, mod, ln_g, ln_b):
    bx, tx, d = x.shape
    tm = min(512, tx)
    _, _, gate = _mod_specs(mod.shape[1], tm)

    def tok(width):
        return pl.BlockSpec((None, tm, width), lambda b, t: (b, t, 0))

    return pl.pallas_call(
        _ssd_out_kernel,
        grid=(bx, tx // tm),
        in_specs=[tok(SSD_D_INNER), tok(SSD_D_INNER), _const_spec((1, SSD_D_INNER)), _const_spec(w_bf.shape),
                  tok(d), gate, _const_spec((1, d)), _const_spec((1, d))],
        out_specs=tok(d),
        out_shape=jax.ShapeDtypeStruct((bx, tx, d), F32),
        scratch_shapes=[pltpu.VMEM((tm, SSD_D_INNER), BF16)],
        compiler_params=_cparams("arbitrary", "arbitrary"),
        name="ssd_out_proj",
    )(y, z, norm_g.reshape(1, SSD_D_INNER), w_bf, x, mod, ln_g.reshape(1, d), ln_b.reshape(1, d))


def _attention_layer(j, lam_init, y_p, y_s, mod_p, mod_s, cache_k, cache_v, page_table, tables,
                     attn_w_qkv, attn_lambda, attn_subln_g, attn_w_o, ln_g, ln_b):
    b, t, d = y_p.shape
    bias_tiles, cfar, mask_far, mask_last, mask_new = tables
    w_qkv = attn_w_qkv[j].astype(BF16)
    w_o = attn_w_o[j].astype(BF16)
    lam_p = attn_lambda[j].astype(F32)
    g = attn_subln_g[j].astype(F32)

    q_bf, k_p, v_p, k_bf, v_bf = _qkv_call(y_p, mod_p, w_qkv)
    o_p = _attn_prompt_call(q_bf, k_bf, v_bf, bias_tiles, cfar, lam_p, g, lam_init)
    y_p = _oproj_call(o_p, w_o, y_p, mod_p, ln_g, ln_b)

    n_seq = page_table.shape[0]
    tq = y_s.shape[1] // n_seq
    n_pool = cache_k.shape[1]
    page_rows = PAGE_SIZE * ATT_HEADS
    pool_k = cache_k.reshape(cache_k.shape[0] * n_pool, page_rows, LANES)
    pool_v = cache_v.reshape(cache_v.shape[0] * n_pool, page_rows, LANES)
    q_bf, k_s, v_s, _, _ = _qkv_call(y_s, mod_s, w_qkv)
    new_rows = lambda a: a.reshape(n_seq, tq * ATT_HEADS, LANES)
    o_s = _attn_sample_call(q_bf.reshape(n_seq, tq, d), new_rows(k_s), new_rows(v_s), pool_k, pool_v,
                            page_table + j * n_pool, mask_far, mask_last, mask_new, lam_p, g, lam_init)
    y_s = _oproj_call(o_s.reshape(1, n_seq * tq, d), w_o, y_s, mod_s, ln_g, ln_b)

    kv = lambda a, n, tt: a.reshape(n, tt, ATT_HEADS, 2 * ATT_HD)
    return y_p, y_s, kv(k_p, b, t), kv(v_p, b, t), kv(k_s, n_seq, tq), kv(v_s, n_seq, tq)


def _ssd_layer(j, y_p, y_s, mod_p, mod_s, n_seq, state_ssm, state_conv, ssd_w_in, ssd_conv_w, ssd_conv_b,
               ssd_dt_bias, ssd_a_log, ssd_d, ssd_norm_g, ssd_w_out, ln_g, ln_b):
    b, t, d = y_p.shape
    tq = y_s.shape[1] // n_seq
    w_in = jnp.pad(ssd_w_in[j], ((0, 0), (0, DT_PAD - SSD_HEADS))).astype(BF16)
    w_out = ssd_w_out[j].astype(BF16)
    scan_w = (ssd_conv_w[j].astype(F32), ssd_conv_b[j].astype(F32), ssd_dt_bias[j], ssd_a_log[j], ssd_d[j])

    def mixer(y, mod, seqs, rows, conv0, ssm0):
        z, xbc, dt = _ssd_in_call(y, mod, w_in)
        per_seq = lambda a: a.reshape(seqs, rows, a.shape[-1])
        yy, fin, nconv = _ssd_scan_call(per_seq(xbc), per_seq(dt), conv0, ssm0, *scan_w)
        yy = yy.reshape(y.shape[0], y.shape[1], SSD_D_INNER)
        y_new = _ssd_out_call(yy, z, ssd_norm_g[j].astype(F32), w_out, y, mod, ln_g, ln_b)
        return y_new, fin.reshape(seqs, SSD_HEADS, SSD_HEADDIM, SSD_STATE), nconv

    zero_conv = jnp.zeros((b, SSD_CONV - 1, SSD_CONV_DIM), F32)
    zero_ssm = jnp.zeros((b, SSD_D_INNER, SSD_STATE), F32)
    y_p, ssm_p, conv_p = mixer(y_p, mod_p, b, t, zero_conv, zero_ssm)
    y_s, ssm_s, conv_s = mixer(y_s, mod_s, n_seq, tq, state_conv[j].astype(F32),
                               state_ssm[j].astype(F32).reshape(n_seq, SSD_D_INNER, SSD_STATE))
    return y_p, y_s, ssm_p, conv_p, ssm_s, conv_s


def kernel(x_prompt, x_sample, cache_k, cache_v, state_ssm, state_conv, page_table, c_prompt, c_sample,
           ada_w, ada_b, ln_g, ln_b, rel_bias, attn_w_qkv, attn_lambda, attn_subln_g, attn_w_o,
           ssd_w_in, ssd_conv_w, ssd_conv_b, ssd_dt_bias, ssd_a_log, ssd_d, ssd_norm_g, ssd_w_out,
           ffn_w_in, ffn_w_out):
    b, t, d = x_prompt.shape
    n_seq, tq, _ = x_sample.shape
    depth = ada_w.shape[0]

    n_c = b + n_seq
    rc = -(-n_c // 8) * 8
    c_all = jnp.concatenate([c_prompt, c_sample, jnp.zeros((rc - n_c, d), F32)], axis=0)
    mod_all = _mod_call(c_all, ada_w.reshape(depth * 2, d, 3 * d), ada_b.reshape(depth * 2, 3 * d))

    def mods(i, s):
        m = mod_all[2 * i + s]
        mod_p = m[0:b].reshape(b, 1, 3 * d)
        mod_s = jnp.repeat(m[b:b + n_seq], tq, axis=0).reshape(1, n_seq * tq, 3 * d)
        return mod_p, mod_s

    tile = min(ATTN_TILE, t)
    past = page_table.shape[1] * PAGE_SIZE
    bias_tiles = jnp.concatenate([_toeplitz_tile(rel_bias, tile, tile), _toeplitz_tile(rel_bias, tile, 0)], axis=2)
    cfar = _bias_by_distance(rel_bias, jnp.asarray([2 * tile], jnp.int32))[:, 0]
    tables = (bias_tiles, cfar) + _sample_masks(rel_bias, tq, past, PAGES_PER_STEP * PAGE_SIZE)

    y_p = x_prompt
    y_s = x_sample.reshape(1, n_seq * tq, d)
    k_p, v_p, k_s, v_s = [], [], [], []
    ssm_p, conv_p, ssm_s, conv_s = [], [], [], []
    for i in range(depth):
        j = i // 2
        mod_p, mod_s = mods(i, 0)
        if i % 2 == 0:
            lam_init = 0.8 - 0.6 * math.exp(-0.3 * i)
            y_p, y_s, kp, vp, ks, vs = _attention_layer(
                j, lam_init, y_p, y_s, mod_p, mod_s, cache_k, cache_v, page_table, tables,
                attn_w_qkv, attn_lambda, attn_subln_g, attn_w_o, ln_g[i, 0], ln_b[i, 0])
            k_p.append(kp)
            v_p.append(vp)
            k_s.append(ks)
            v_s.append(vs)
        else:
            y_p, y_s, sp, cp, ss, cs = _ssd_layer(
                j, y_p, y_s, mod_p, mod_s, n_seq, state_ssm, state_conv, ssd_w_in, ssd_conv_w, ssd_conv_b,
                ssd_dt_bias, ssd_a_log, ssd_d, ssd_norm_g, ssd_w_out, ln_g[i, 0], ln_b[i, 0])
            ssm_p.append(sp)
            conv_p.append(cp)
            ssm_s.append(ss)
            conv_s.append(cs)
        mod_p, mod_s = mods(i, 1)
        wi = ffn_w_in[i].astype(BF16)
        wo = ffn_w_out[i].astype(BF16)
        y_p = _ffn_call(y_p, mod_p, wi, wo, ln_g[i, 1], ln_b[i, 1])
        y_s = _ffn_call(y_s, mod_s, wi, wo, ln_g[i, 1], ln_b[i, 1])
    return (y_p, y_s.reshape(n_seq, tq, d), jnp.stack(k_p), jnp.stack(v_p), jnp.stack(k_s), jnp.stack(v_s),
            jnp.stack(ssm_p), jnp.stack(conv_p), jnp.stack(ssm_s), jnp.stack(conv_s))
```

```python
import functools
import math

import jax
import jax.numpy as jnp
from jax import lax
from jax.experimental import pallas as pl
from jax.experimental.pallas import tpu as pltpu

F32 = jnp.float32
BF16 = jnp.bfloat16

D_MODEL = 1024
DEPTH = 4
PAGE_SIZE = 128
ATT_HEADS = 8
ATT_HD = 64
ATT_VD = 128
REL_BUCKETS = 32
REL_MAX_DIST = 128
SSD_D_INNER = 2048
SSD_HEADDIM = 64
SSD_HEADS = 32
SSD_GROUPS = 4
SSD_STATE = 128
SSD_CONV = 4
SSD_GN = SSD_GROUPS * SSD_STATE
SSD_CONV_DIM = SSD_D_INNER + 2 * SSD_GN
SSD_CHUNK = 128
FFN_HIDDEN = 2816
DEEPNORM_ALPHA = (2 * DEPTH) ** 0.25
LN_EPS = 1e-5
RMS_EPS = 1e-5

LANES = 128
DT_PAD = LANES
NEG_BIG = -1e30
LOG2E = math.log2(math.e)
VMEM_LIMIT = 56 * 1024 * 1024
ATTN_TILE = 512
ATTN_ROWS = 128
ATTN_NEAR_ROWS = 256
FAR_TILES_PER_STEP = 4
ATTN_HEADS_PER_STEP = 2
BAND_MARGIN = REL_MAX_DIST
PAGES_PER_STEP = 8
FFN_CHUNK = 256


def _cparams(*sem):
    return pltpu.CompilerParams(dimension_semantics=sem, vmem_limit_bytes=VMEM_LIMIT)


def _silu(x):
    return x * jax.nn.sigmoid(x)


def _layer_norm(r, g, b):
    mu = jnp.mean(r, axis=-1, keepdims=True)
    d = r - mu
    var = jnp.mean(d * d, axis=-1, keepdims=True)
    return d * lax.rsqrt(var + LN_EPS) * g + b


def _mod_specs(mod_rows, tm):
    rows = 1 if mod_rows == 1 else tm

    def spec(piece):
        if mod_rows == 1:
            return pl.BlockSpec((None, 1, D_MODEL), lambda b, t: (b, 0, piece))
        return pl.BlockSpec((None, rows, D_MODEL), lambda b, t: (b, t, piece))

    return spec(0), spec(1), spec(2)


def _const_spec(shape):
    n = len(shape)
    return pl.BlockSpec(shape, lambda b, t: (0,) * n)


def _mod_kernel(c_ref, w_ref, b_ref, o_ref):
    a = _silu(c_ref[...]).astype(BF16)
    o_ref[...] = jnp.dot(a, w_ref[...].astype(BF16), preferred_element_type=F32) + b_ref[...]


def _mod_call(c_all, ada_w, ada_b):
    n_sub, d, d3 = ada_w.shape
    rc = c_all.shape[0]
    tn = 1024
    return pl.pallas_call(
        _mod_kernel,
        grid=(n_sub, d3 // tn),
        in_specs=[
            pl.BlockSpec((rc, d), lambda i, j: (0, 0)),
            pl.BlockSpec((None, d, tn), lambda i, j: (i, 0, j)),
            pl.BlockSpec((None, 1, tn), lambda i, j: (i, 0, j)),
        ],
        out_specs=pl.BlockSpec((None, rc, tn), lambda i, j: (i, 0, j)),
        out_shape=jax.ShapeDtypeStruct((n_sub, rc, d3), F32),
        compiler_params=_cparams("arbitrary", "arbitrary"),
        name="adaln_mod",
    )(c_all, ada_w, ada_b.reshape(n_sub, 1, d3))


def _qkv_kernel(x_ref, sh_ref, sc_ref, w_ref, q_ref, k_ref, v_ref, kb_ref, vb_ref):
    d = D_MODEL
    h = (x_ref[...] * (1.0 + sc_ref[...]) + sh_ref[...]).astype(BF16)
    q = jnp.dot(h, w_ref[:, 0:d], preferred_element_type=F32)
    q_ref[...] = (q * (ATT_HD ** -0.5 * LOG2E)).astype(BF16)
    tm = x_ref.shape[0]
    k = jnp.dot(h, w_ref[:, d:2 * d], preferred_element_type=F32)
    kb_ref[...] = k.astype(BF16)
    v = jnp.dot(h, w_ref[:, 2 * d:3 * d], preferred_element_type=F32)
    vb_ref[...] = v.astype(BF16)
    for hd in range(ATT_HEADS):
        hs = slice(hd * LANES, (hd + 1) * LANES)
        k_ref[pl.ds(hd, tm, stride=ATT_HEADS), :] = k[:, hs]
        v_ref[pl.ds(hd, tm, stride=ATT_HEADS), :] = v[:, hs]


def _qkv_call(x, mod, w_bf):
    bx, tx, d = x.shape
    tm = min(512, tx)
    sh, sc, _ = _mod_specs(mod.shape[1], tm)
    tok = pl.BlockSpec((None, tm, d), lambda b, t: (b, t, 0))
    rows = pl.BlockSpec((None, tm * ATT_HEADS, LANES), lambda b, t: (b, t, 0))
    return pl.pallas_call(
        _qkv_kernel,
        grid=(bx, tx // tm),
        in_specs=[tok, sh, sc, _const_spec(w_bf.shape)],
        out_specs=[tok, rows, rows, tok, tok],
        out_shape=[
            jax.ShapeDtypeStruct((bx, tx, d), BF16),
            jax.ShapeDtypeStruct((bx, tx * ATT_HEADS, LANES), F32),
            jax.ShapeDtypeStruct((bx, tx * ATT_HEADS, LANES), F32),
            jax.ShapeDtypeStruct((bx, tx, d), BF16),
            jax.ShapeDtypeStruct((bx, tx, d), BF16),
        ],
        compiler_params=_cparams("arbitrary", "arbitrary"),
        name="qkv_proj",
    )(x, mod, mod, w_bf)


def _stack_maps(q):
    lane = lax.broadcasted_iota(jnp.int32, q.shape, 1)
    zero = jnp.zeros_like(q)
    return jnp.concatenate([jnp.where(lane < ATT_HD, q, zero), jnp.where(lane >= ATT_HD, q, zero)], axis=0)


def _lambda_full(lam_ref, lam_init):
    lp = lam_ref[...]
    a = jnp.sum(lp[0:1, :] * lp[1:2, :], axis=-1, keepdims=True)
    b = jnp.sum(lp[2:3, :] * lp[3:4, :], axis=-1, keepdims=True)
    return jnp.exp(a) - jnp.exp(b) + lam_init


def _diff_finish(num, den, t, lam, g, lam_init):
    o = num[0:t] / den[0:t] - lam * (num[t:2 * t] / den[t:2 * t])
    ms = jnp.mean(o * o, axis=-1, keepdims=True)
    return o * lax.rsqrt(ms + RMS_EPS) * g * (1.0 - lam_init)


def _nt_dot(a, b):
    return lax.dot_general(a, b, (((1,), (1,)), ((), ())), preferred_element_type=F32)


def _lane_tile(x, n):
    return jnp.concatenate([x] * n, axis=1)


def _online_softmax(segments, v_ext, m_prev, acc_prev):
    m_next = m_prev
    for s, c in segments:
        m_curr = jnp.max(s, axis=1, keepdims=True)
        m_next = jnp.maximum(m_next, m_curr if c is None else m_curr + c)
    ps = []
    for s, c in segments:
        shift = m_next if c is None else m_next - c
        ps.append(jnp.exp2(s - _lane_tile(shift, s.shape[1] // LANES)).astype(BF16))
    p = ps[0] if len(ps) == 1 else jnp.concatenate(ps, axis=1)
    alpha = jnp.exp2(m_prev - m_next)
    acc = _lane_tile(alpha, 2) * acc_prev + jnp.dot(p, v_ext, preferred_element_type=F32)
    return m_next, acc


def _attn_prompt_kernel(q_ref, k_ref, v_ref, band_ref, cfar_ref, lam_ref, g_ref, o_ref,
                        qq_sc, vext_sc, m_sc, acc_sc, *, t, lam_init):
    hp = ATTN_HEADS_PER_STEP
    head0 = pl.program_id(1) * hp
    qi = pl.program_id(2)
    rs = min(ATTN_ROWS, t)

    @pl.when(qi == 0)
    def _():
        for hh in range(hp):
            vext_sc[hh, :, 0:ATT_VD] = v_ref[:, hh * LANES:(hh + 1) * LANES]
            vext_sc[hh, :, ATT_VD:2 * ATT_VD] = jnp.ones((vext_sc.shape[1], ATT_VD), BF16)

    for hh in range(hp):
        qq_sc[hh] = _stack_maps(q_ref[:, hh * LANES:(hh + 1) * LANES])
    m_sc[...] = jnp.full(m_sc.shape, NEG_BIG, F32)
    acc_sc[...] = jnp.zeros(acc_sc.shape, F32)

    def chains(rs):
        for hh in range(hp):
            for r in range(2 * t // rs):
                yield hh, slice(hh * LANES, (hh + 1) * LANES), slice(r * rs, (r + 1) * rs), (r * rs) % t

    def tile_update(ki, width):
        start = pl.multiple_of(ki * t, t)
        for hh, hs, rows, _ in chains(rs):
            s = _nt_dot(qq_sc[hh, rows, :], k_ref[pl.ds(start, width), hs])
            m, acc = _online_softmax([(s, cfar_ref[head0 + hh])], vext_sc[hh, pl.ds(start, width), :],
                                     m_sc[hh, rows, :], acc_sc[hh, rows, :])
            m_sc[hh, rows, :] = m
            acc_sc[hh, rows, :] = acc

    def diag_update(base, d0):
        rs = band_ref.shape[1]
        for hh, hs, rows, b0 in chains(rs):
            hi = d0 + b0 + rs
            lo = max(hi - rs - BAND_MARGIN, 0)
            band0 = rs + BAND_MARGIN - (hi - lo)
            q = qq_sc[hh, rows, :]
            near0 = base + lo if isinstance(base, int) else pl.multiple_of(base + lo, LANES)
            s_near = _nt_dot(q, k_ref[pl.ds(near0, hi - lo), hs])
            segments = [(s_near + band_ref[hh, :, band0:band0 + hi - lo], None)]
            if lo > 0:
                segments.insert(0, (_nt_dot(q, k_ref[pl.ds(base, lo), hs]), cfar_ref[head0 + hh]))
            m, acc = _online_softmax(segments, vext_sc[hh, pl.ds(base, hi), :], m_sc[hh, rows, :], acc_sc[hh, rows, :])
            m_sc[hh, rows, :] = m
            acc_sc[hh, rows, :] = acc

    n_far = jnp.maximum(qi - 1, 0)
    n_wide = n_far // FAR_TILES_PER_STEP

    def wide_body(i, carry):
        tile_update(i * FAR_TILES_PER_STEP, FAR_TILES_PER_STEP * t)
        return carry

    def far_body(ki, carry):
        tile_update(ki, t)
        return carry

    lax.fori_loop(0, n_wide, wide_body, 0)
    lax.fori_loop(n_wide * FAR_TILES_PER_STEP, n_far, far_body, 0)

    @pl.when(qi >= 1)
    def _():
        diag_update(pl.multiple_of((qi - 1) * t, t), t)

    @pl.when(qi == 0)
    def _():
        diag_update(0, 0)

    lam = _lambda_full(lam_ref, lam_init)
    for hh in range(hp):
        acc = acc_sc[hh]
        o_ref[:, hh * LANES:(hh + 1) * LANES] = _diff_finish(
            acc[:, 0:ATT_VD], acc[:, ATT_VD:2 * ATT_VD], t, lam, g_ref[...], lam_init).astype(BF16)


def _attn_prompt_call(q_bf, k_bf, v_bf, band, cfar, lam_p, g, lam_init):
    b, tt, d = q_bf.shape
    t = min(ATTN_TILE, tt)
    hp = ATTN_HEADS_PER_STEP
    qtile = pl.BlockSpec((None, t, hp * LANES), lambda b, h, q: (b, q, h))
    heads = pl.BlockSpec((None, tt, hp * LANES), lambda b, h, q: (b, 0, h))
    return pl.pallas_call(
        functools.partial(_attn_prompt_kernel, t=t, lam_init=lam_init),
        grid=(b, ATT_HEADS // hp, tt // t),
        in_specs=[
            qtile, heads, heads,
            pl.BlockSpec((hp,) + band.shape[1:], lambda b, h, q: (h, 0, 0)),
            pl.BlockSpec(memory_space=pltpu.SMEM),
            pl.BlockSpec((4, ATT_HD), lambda b, h, q: (0, 0)),
            pl.BlockSpec((1, ATT_VD), lambda b, h, q: (0, 0)),
        ],
        out_specs=qtile,
        out_shape=jax.ShapeDtypeStruct((b, tt, d), BF16),
        scratch_shapes=[
            pltpu.VMEM((hp, 2 * t, LANES), BF16),
            pltpu.VMEM((hp, tt, 2 * ATT_VD), BF16),
            pltpu.VMEM((hp, 2 * t, LANES), F32),
            pltpu.VMEM((hp, 2 * t, 2 * ATT_VD), F32),
        ],
        compiler_params=_cparams("arbitrary", "arbitrary", "arbitrary"),
        name="attn_prompt",
    )(q_bf, k_bf, v_bf, band, cfar, lam_p, g.reshape(1, ATT_VD))


def _attn_sample_kernel(pt_ref, q_ref, kn_ref, vn_ref, *rest, tq, n_pages, lam_init):
    k_pages = rest[:n_pages]
    v_pages = rest[n_pages:2 * n_pages]
    (mask_far_ref, mask_last_ref, mask_new_ref, lam_ref, g_ref, o_ref,
     qq_sc, m_sc, acc_sc) = rest[2 * n_pages:]
    c = pl.program_id(1)
    last = pl.num_programs(1) - 1
    page_rows = PAGE_SIZE * ATT_HEADS

    @pl.when(c == 0)
    def _():
        q = q_ref[...].astype(F32)
        for h in range(ATT_HEADS):
            qq_sc[h * 2 * tq:(h + 1) * 2 * tq, :] = _stack_maps(q[:, h * LANES:(h + 1) * LANES]).astype(BF16)
        m_sc[...] = jnp.full(m_sc.shape, NEG_BIG, F32)
        acc_sc[...] = jnp.zeros(acc_sc.shape, F32)

    def update(k_rows, v_rows, mask):
        ones = jnp.ones((v_rows.shape[0], ATT_VD), BF16)
        v_ext = jnp.concatenate([v_rows.astype(BF16), ones], axis=1)
        s = _nt_dot(qq_sc[...], k_rows.astype(BF16)) + mask
        m, acc = _online_softmax([(s, None)], v_ext, m_sc[...], acc_sc[...])
        m_sc[...] = m
        acc_sc[...] = acc

    @pl.when(c < last)
    def _():
        for j in range(n_pages):
            update(k_pages[j][...], v_pages[j][...], mask_far_ref[...])

    @pl.when(c == last)
    def _():
        for j in range(n_pages):
            update(k_pages[j][...], v_pages[j][...], mask_last_ref[:, j * page_rows:(j + 1) * page_rows])
        rows_new = kn_ref.shape[0]
        pad = jnp.zeros((LANES - rows_new, LANES), F32)
        update(jnp.concatenate([kn_ref[...], pad], axis=0), jnp.concatenate([vn_ref[...], pad], axis=0),
               mask_new_ref[...])
        lam = _lambda_full(lam_ref, lam_init)
        acc = acc_sc[...]
        for h in range(ATT_HEADS):
            rows = slice(h * 2 * tq, (h + 1) * 2 * tq)
            o_ref[:, h * LANES:(h + 1) * LANES] = _diff_finish(
                acc[rows, 0:ATT_VD], acc[rows, ATT_VD:2 * ATT_VD], tq, lam, g_ref[...], lam_init).astype(BF16)


def _attn_sample_call(q_bf, k_new, v_new, pool_k, pool_v, pages, mask_far, mask_last, mask_new,
                      lam_p, g, lam_init):
    bs, tq, d = q_bf.shape
    n_pages_total = pages.shape[1]
    pp = PAGES_PER_STEP
    n_chunks = n_pages_total // pp
    page_rows = PAGE_SIZE * ATT_HEADS
    n_rows = ATT_HEADS * 2 * tq
    tok = pl.BlockSpec((None, tq, d), lambda b, c, pt: (b, 0, 0))
    new = pl.BlockSpec((None, tq * ATT_HEADS, LANES), lambda b, c, pt: (b, 0, 0))

    def page_spec(j):
        return pl.BlockSpec((None, page_rows, LANES), lambda b, c, pt: (pt[b, c * pp + j], 0, 0))

    def const(shape):
        n = len(shape)
        return pl.BlockSpec(shape, lambda b, c, pt: (0,) * n)

    grid_spec = pltpu.PrefetchScalarGridSpec(
        num_scalar_prefetch=1,
        grid=(bs, n_chunks),
        in_specs=[tok, new, new]
        + [page_spec(j) for j in range(pp)]
        + [page_spec(j) for j in range(pp)]
        + [const(mask_far.shape), const(mask_last.shape), const(mask_new.shape),
           const((4, ATT_HD)), const((1, ATT_VD))],
        out_specs=tok,
        scratch_shapes=[
            pltpu.VMEM((n_rows, LANES), BF16),
            pltpu.VMEM((n_rows, LANES), F32),
            pltpu.VMEM((n_rows, 2 * ATT_VD), F32),
        ],
    )
    return pl.pallas_call(
        functools.partial(_attn_sample_kernel, tq=tq, n_pages=pp, lam_init=lam_init),
        grid_spec=grid_spec,
        out_shape=jax.ShapeDtypeStruct((bs, tq, d), BF16),
        compiler_params=_cparams("arbitrary", "arbitrary"),
        name="attn_sample",
    )(pages, q_bf, k_new, v_new, *([pool_k] * pp), *([pool_v] * pp), mask_far, mask_last, mask_new,
      lam_p, g.reshape(1, ATT_VD))


def _bias_by_distance(rel_bias, n):
    max_exact = REL_BUCKETS // 2
    nn = jnp.maximum(n, 0)
    nf = jnp.maximum(nn, 1).astype(F32)
    large = max_exact + (jnp.log(nf / max_exact) / math.log(REL_MAX_DIST / max_exact)
                         * (REL_BUCKETS - max_exact)).astype(jnp.int32)
    bucket = jnp.where(nn < max_exact, nn, jnp.minimum(large, REL_BUCKETS - 1))
    table = (rel_bias.astype(F32) * LOG2E).T
    onehot = (bucket[None, :] == jnp.arange(REL_BUCKETS)[:, None]).astype(F32)
    vals = jnp.dot(table, onehot, precision=lax.Precision.HIGHEST)
    return jnp.where((n >= 0)[None, :], vals, NEG_BIG)


def _diagonal_band(rel_bias, rows):
    dist = BAND_MARGIN + jnp.arange(rows)[:, None] - jnp.arange(rows + BAND_MARGIN)[None, :]
    return _bias_by_distance(rel_bias, dist.reshape(-1)).reshape(-1, rows, rows + BAND_MARGIN)


def _sample_masks(rel_bias, tq, past, span):
    h = ATT_HEADS
    same = (jnp.arange(h)[:, None] == jnp.arange(h)[None, :])
    cfar = _bias_by_distance(rel_bias, jnp.asarray([past], jnp.int32))[:, 0]

    def expand(base):
        n_tok = base.shape[-1]
        full = jnp.where(same[:, None, None, None, :], base[:, None, :, :, None], NEG_BIG)
        full = jnp.broadcast_to(full, (h, 2, tq, n_tok, h))
        return full.reshape(h * 2 * tq, n_tok * h)

    i = jnp.arange(tq)
    far = expand(jnp.broadcast_to(cfar[:, None, None], (h, tq, PAGE_SIZE)))
    j = jnp.arange(span)
    d_last = (span + i[:, None] - j[None, :]).reshape(-1)
    last = expand(_bias_by_distance(rel_bias, d_last).reshape(h, tq, span))
    n_new = LANES // h
    jn = jnp.arange(n_new)
    d_new = jnp.where(jn[None, :] < tq, i[:, None] - jn[None, :], -1).reshape(-1)
    new = expand(_bias_by_distance(rel_bias, d_new).reshape(h, tq, n_new))
    return far, last, new


def _oproj_kernel(a_ref, w_ref, x_ref, gate_ref, g_ref, b_ref, o_ref):
    out = jnp.dot(a_ref[...], w_ref[...], preferred_element_type=F32)
    r = DEEPNORM_ALPHA * x_ref[...] + gate_ref[...] * out
    o_ref[...] = _layer_norm(r, g_ref[...], b_ref[...])


def _oproj_call(a_bf, w_bf, x, mod, ln_g, ln_b):
    bx, tx, d = x.shape
    dk = a_bf.shape[-1]
    tm = min(512, tx)
    _, _, gate = _mod_specs(mod.shape[1], tm)
    tok = pl.BlockSpec((None, tm, d), lambda b, t: (b, t, 0))
    return pl.pallas_call(
        _oproj_kernel,
        grid=(bx, tx // tm),
        in_specs=[pl.BlockSpec((None, tm, dk), lambda b, t: (b, t, 0)), _const_spec(w_bf.shape), tok, gate,
                  _const_spec((1, d)), _const_spec((1, d))],
        out_specs=tok,
        out_shape=jax.ShapeDtypeStruct((bx, tx, d), F32),
        compiler_params=_cparams("arbitrary", "arbitrary"),
        name="attn_out_proj",
    )(a_bf, w_bf, x, mod, ln_g.reshape(1, d), ln_b.reshape(1, d))


def _ffn_kernel(x_ref, sh_ref, sc_ref, gate_ref, wi_ref, wo_ref, g_ref, b_ref, o_ref, acc_ref):
    x = x_ref[...]
    h = (x * (1.0 + sc_ref[...]) + sh_ref[...]).astype(BF16)
    for c in range(FFN_HIDDEN // FFN_CHUNK):
        lo = c * FFN_CHUNK
        gte = jnp.dot(h, wi_ref[:, lo:lo + FFN_CHUNK], preferred_element_type=F32)
        up = jnp.dot(h, wi_ref[:, FFN_HIDDEN + lo:FFN_HIDDEN + lo + FFN_CHUNK], preferred_element_type=F32)
        a = (_silu(gte) * up).astype(BF16)
        part = jnp.dot(a, wo_ref[lo:lo + FFN_CHUNK, :], preferred_element_type=F32)
        if c == 0:
            acc_ref[...] = part
        else:
            acc_ref[...] += part
    r = DEEPNORM_ALPHA * x + gate_ref[...] * acc_ref[...]
    o_ref[...] = _layer_norm(r, g_ref[...], b_ref[...])


def _ffn_call(x, mod, wi_bf, wo_bf, ln_g, ln_b):
    bx, tx, d = x.shape
    tm = min(512, tx)
    sh, sc, gate = _mod_specs(mod.shape[1], tm)
    tok = pl.BlockSpec((None, tm, d), lambda b, t: (b, t, 0))
    return pl.pallas_call(
        _ffn_kernel,
        grid=(bx, tx // tm),
        in_specs=[tok, sh, sc, gate, _const_spec(wi_bf.shape), _const_spec(wo_bf.shape),
                  _const_spec((1, d)), _const_spec((1, d))],
        out_specs=tok,
        out_shape=jax.ShapeDtypeStruct((bx, tx, d), F32),
        scratch_shapes=[pltpu.VMEM((tm, d), F32)],
        compiler_params=_cparams("arbitrary", "arbitrary"),
        name="ffn",
    )(x, mod, mod, mod, wi_bf, wo_bf, ln_g.reshape(1, d), ln_b.reshape(1, d))


def _ssd_in_kernel(x_ref, sh_ref, sc_ref, w_ref, z_ref, xbc_ref, dt_ref):
    h = (x_ref[...] * (1.0 + sc_ref[...]) + sh_ref[...]).astype(BF16)
    di = SSD_D_INNER
    for lo in range(0, di, 1024):
        z_ref[:, lo:lo + 1024] = jnp.dot(h, w_ref[:, lo:lo + 1024], preferred_element_type=F32)
    for lo in range(0, SSD_CONV_DIM, 1024):
        xbc_ref[:, lo:lo + 1024] = jnp.dot(h, w_ref[:, di + lo:di + lo + 1024], preferred_element_type=F32)
    lo = di + SSD_CONV_DIM
    dt_ref[...] = jnp.dot(h, w_ref[:, lo:lo + DT_PAD], preferred_element_type=F32)


def _ssd_in_call(x, mod, w_bf):
    bx, tx, d = x.shape
    tm = min(256, tx)
    sh, sc, _ = _mod_specs(mod.shape[1], tm)

    def tok(width):
        return pl.BlockSpec((None, tm, width), lambda b, t: (b, t, 0))

    return pl.pallas_call(
        _ssd_in_kernel,
        grid=(bx, tx // tm),
        in_specs=[tok(d), sh, sc, _const_spec(w_bf.shape)],
        out_specs=[tok(SSD_D_INNER), tok(SSD_CONV_DIM), tok(DT_PAD)],
        out_shape=[
            jax.ShapeDtypeStruct((bx, tx, SSD_D_INNER), F32),
            jax.ShapeDtypeStruct((bx, tx, SSD_CONV_DIM), F32),
            jax.ShapeDtypeStruct((bx, tx, DT_PAD), F32),
        ],
        compiler_params=_cparams("arbitrary", "arbitrary"),
        name="ssd_in_proj",
    )(x, mod, mod, w_bf)


def _ssd_scan_kernel(xbc_ref, dt_ref, conv0_ref, ssm0_ref, cw_ref, cb_ref, dtb_ref, alog_ref, dsk_ref,
                     y_ref, fin_ref, nconv_ref, ext_sc, xs_sc, st_sc, *, rows):
    L = SSD_CHUNK
    di = SSD_D_INNER
    c = pl.program_id(1)
    last = pl.num_programs(1) - 1
    tail = 8

    @pl.when(c == 0)
    def _():
        ext_sc[0:tail, :] = jnp.zeros((tail, SSD_CONV_DIM), F32)
        ext_sc[tail - (SSD_CONV - 1):tail, :] = conv0_ref[...]
        st_sc[...] = ssm0_ref[...]

    ext_sc[tail:tail + rows, :] = xbc_ref[...]
    if rows < L:
        ext_sc[tail + rows:tail + L, :] = jnp.zeros((L - rows, SSD_CONV_DIM), F32)

    @pl.when(c == last)
    def _():
        nconv_ref[...] = ext_sc[tail + rows - (SSD_CONV - 1):tail + rows, :]

    cblk = 512
    for lo in range(0, SSD_CONV_DIM, cblk):
        acc = cb_ref[:, lo:lo + cblk] + ext_sc[tail:tail + L, lo:lo + cblk] * cw_ref[3:4, lo:lo + cblk]
        for k in range(1, SSD_CONV):
            w_row = cw_ref[3 - k:4 - k, lo:lo + cblk]
            acc = acc + ext_sc[tail - k:tail - k + L, lo:lo + cblk] * w_row
        xs_sc[:, lo:lo + cblk] = _silu(acc)

    ext_sc[0:tail, :] = ext_sc[L:L + tail, :]

    if rows < L:
        dt_raw = jnp.concatenate([dt_ref[...], jnp.zeros((L - rows, DT_PAD), F32)], axis=0)
    else:
        dt_raw = dt_ref[...]
    xx = dt_raw + dtb_ref[...]
    dt = jnp.maximum(xx, 0.0) + jnp.log1p(jnp.exp(-jnp.abs(xx)))
    if rows < L:
        row = lax.broadcasted_iota(jnp.int32, (L, DT_PAD), 0)
        dt = jnp.where(row < rows, dt, 0.0)
    adt = -jnp.exp(alog_ref[...]) * dt
    ri = lax.broadcasted_iota(jnp.int32, (L, L), 0)
    ci = lax.broadcasted_iota(jnp.int32, (L, L), 1)
    causal = ci <= ri
    tril = jnp.where(causal, 1.0, 0.0).astype(F32)
    acs = jnp.dot(tril, adt, preferred_element_type=F32, precision=lax.Precision.HIGHEST)
    acs_t = acs.T
    dt_t = dt.T
    acs_end_t = acs_t[:, L - 1:L]
    w_t = dt_t * jnp.exp(acs_end_t - acs_t)
    cdec_t = jnp.exp(acs_end_t)

    half = lax.broadcasted_iota(jnp.int32, (L, LANES), 1) < SSD_HEADDIM
    for g in range(SSD_GROUPS):
        bm = xs_sc[:, di + g * SSD_STATE:di + (g + 1) * SSD_STATE]
        cm = xs_sc[:, di + SSD_GN + g * SSD_STATE:di + SSD_GN + (g + 1) * SSD_STATE]
        bm_bf = bm.astype(BF16)
        cb = _nt_dot(cm.astype(BF16), bm_bf)
        for jj in range(SSD_HEADS // SSD_GROUPS // 2):
            j = g * (SSD_HEADS // SSD_GROUPS // 2) + jj
            ps = slice(j * LANES, (j + 1) * LANES)
            x_pair = xs_sc[:, ps]
            x_bf = x_pair.astype(BF16)
            st_pair = st_sc[ps, :]
            st_bf = st_pair.astype(BF16)
            ys = []
            for hh in range(2):
                hd = 2 * j + hh
                acs_col = jnp.broadcast_to(acs[:, hd:hd + 1], (L, L))
                seg = acs_col - acs_t[hd:hd + 1, :]
                dec = jnp.exp(jnp.where(causal, seg, -jnp.inf))
                mh = (cb * dec * dt_t[hd:hd + 1, :]).astype(BF16)
                y_h = jnp.dot(mh, x_bf, preferred_element_type=F32)
                cme = (cm * jnp.exp(acs_col)).astype(BF16)
                y_h = y_h + _nt_dot(cme, st_bf)
                ys.append(y_h)
            y_pair = jnp.where(half, ys[0], ys[1]) + x_pair * dsk_ref[:, ps]
            y_ref[:, ps] = y_pair[0:rows]
            x_t = x_pair.T
            w_rows = jnp.concatenate([jnp.broadcast_to(w_t[2 * j:2 * j + 1, :], (SSD_HEADDIM, L)),
                                      jnp.broadcast_to(w_t[2 * j + 1:2 * j + 2, :], (SSD_HEADDIM, L))], axis=0)
            d_rows = jnp.concatenate([jnp.broadcast_to(cdec_t[2 * j:2 * j + 1, :], (SSD_HEADDIM, 1)),
                                      jnp.broadcast_to(cdec_t[2 * j + 1:2 * j + 2, :], (SSD_HEADDIM, 1))], axis=0)
            upd = jnp.dot((x_t * w_rows).astype(BF16), bm_bf, preferred_element_type=F32)
            st_sc[ps, :] = st_pair * d_rows + upd

    @pl.when(c == last)
    def _():
        fin_ref[...] = st_sc[...]


def _ssd_scan_call(xbc, dt, conv0, ssm0, conv_w, conv_b, dt_bias, a_log, d_skip):
    bs, tt, _ = xbc.shape
    rows = SSD_CHUNK if tt % SSD_CHUNK == 0 else tt
    nc = tt // rows
    heads_pad = DT_PAD - SSD_HEADS
    dtb = jnp.pad(dt_bias.astype(F32), (0, heads_pad)).reshape(1, DT_PAD)
    alog = jnp.pad(a_log.astype(F32), (0, heads_pad)).reshape(1, DT_PAD)
    dsk = jnp.repeat(d_skip.astype(F32), SSD_HEADDIM).reshape(1, SSD_D_INNER)

    def tok(width):
        return pl.BlockSpec((None, rows, width), lambda b, c: (b, c, 0))

    def per_seq(r, width):
        return pl.BlockSpec((None, r, width), lambda b, c: (b, 0, 0))

    return pl.pallas_call(
        functools.partial(_ssd_scan_kernel, rows=rows),
        grid=(bs, nc),
        in_specs=[tok(SSD_CONV_DIM), tok(DT_PAD), per_seq(SSD_CONV - 1, SSD_CONV_DIM),
                  per_seq(SSD_D_INNER, SSD_STATE),
                  _const_spec((SSD_CONV, SSD_CONV_DIM)), _const_spec((1, SSD_CONV_DIM)),
                  _const_spec((1, DT_PAD)), _const_spec((1, DT_PAD)), _const_spec((1, SSD_D_INNER))],
        out_specs=[tok(SSD_D_INNER), per_seq(SSD_D_INNER, SSD_STATE), per_seq(SSD_CONV - 1, SSD_CONV_DIM)],
        out_shape=[
            jax.ShapeDtypeStruct((bs, tt, SSD_D_INNER), F32),
            jax.ShapeDtypeStruct((bs, SSD_D_INNER, SSD_STATE), F32),
            jax.ShapeDtypeStruct((bs, SSD_CONV - 1, SSD_CONV_DIM), F32),
        ],
        scratch_shapes=[
            pltpu.VMEM((SSD_CHUNK + 8, SSD_CONV_DIM), F32),
            pltpu.VMEM((SSD_CHUNK, SSD_CONV_DIM), F32),
            pltpu.VMEM((SSD_D_INNER, SSD_STATE), F32),
        ],
        compiler_params=_cparams("arbitrary", "arbitrary"),
        name="ssd_scan",
    )(xbc, dt, conv0, ssm0, conv_w, conv_b.reshape(1, SSD_CONV_DIM), dtb, alog, dsk)


def _ssd_out_kernel(y_ref, z_ref, ng_ref, w_ref, x_ref, gate_ref, g_ref, b_ref, o_ref, a_sc):
    gw = SSD_D_INNER // SSD_GROUPS
    for g in range(SSD_GROUPS):
        gs = slice(g * gw, (g + 1) * gw)
        yz = y_ref[:, gs] * _silu(z_ref[:, gs])
        ms = jnp.mean(yz * yz, axis=-1, keepdims=True)
        a_sc[:, gs] = (yz * lax.rsqrt(ms + RMS_EPS) * ng_ref[:, gs]).astype(BF16)
    out = jnp.dot(a_sc[...], w_ref[...], preferred_element_type=F32)
    r = DEEPNORM_ALPHA * x_ref[...] + gate_ref[...] * out
    o_ref[...] = _layer_norm(r, g_ref[...], b_ref[...])


def _ssd_out_call(y, z, norm_g, w_bf, x, mod, ln_g, ln_b):
    bx, tx, d = x.shape
    tm = min(512, tx)
    _, _, gate = _mod_specs(mod.shape[1], tm)

    def tok(width):
        return pl.BlockSpec((None, tm, width), lambda b, t: (b, t, 0))

    return pl.pallas_call(
        _ssd_out_kernel,
        grid=(bx, tx // tm),
        in_specs=[tok(SSD_D_INNER), tok(SSD_D_INNER), _const_spec((1, SSD_D_INNER)), _const_spec(w_bf.shape),
                  tok(d), gate, _const_spec((1, d)), _const_spec((1, d))],
        out_specs=tok(d),
        out_shape=jax.ShapeDtypeStruct((bx, tx, d), F32),
        scratch_shapes=[pltpu.VMEM((tm, SSD_D_INNER), BF16)],
        compiler_params=_cparams("arbitrary", "arbitrary"),
        name="ssd_out_proj",
    )(y, z, norm_g.reshape(1, SSD_D_INNER), w_bf, x, mod, ln_g.reshape(1, d), ln_b.reshape(1, d))


def _attention_layer(j, lam_init, y_p, y_s, mod_p, mod_s, cache_k, cache_v, page_table, tables,
                     attn_w_qkv, attn_lambda, attn_subln_g, attn_w_o, ln_g, ln_b):
    b, t, d = y_p.shape
    band, cfar, mask_far, mask_last, mask_new = tables
    w_qkv = attn_w_qkv[j].astype(BF16)
    w_o = attn_w_o[j].astype(BF16)
    lam_p = attn_lambda[j].astype(F32)
    g = attn_subln_g[j].astype(F32)

    q_bf, k_p, v_p, k_bf, v_bf = _qkv_call(y_p, mod_p, w_qkv)
    o_p = _attn_prompt_call(q_bf, k_bf, v_bf, band, cfar, lam_p, g, lam_init)
    y_p = _oproj_call(o_p, w_o, y_p, mod_p, ln_g, ln_b)

    n_seq = page_table.shape[0]
    tq = y_s.shape[1] // n_seq
    n_pool = cache_k.shape[1]
    page_rows = PAGE_SIZE * ATT_HEADS
    pool_k = cache_k.reshape(cache_k.shape[0] * n_pool, page_rows, LANES)
    pool_v = cache_v.reshape(cache_v.shape[0] * n_pool, page_rows, LANES)
    q_bf, k_s, v_s, _, _ = _qkv_call(y_s, mod_s, w_qkv)
    new_rows = lambda a: a.reshape(n_seq, tq * ATT_HEADS, LANES)
    o_s = _attn_sample_call(q_bf.reshape(n_seq, tq, d), new_rows(k_s), new_rows(v_s), pool_k, pool_v,
                            page_table + j * n_pool, mask_far, mask_last, mask_new, lam_p, g, lam_init)
    y_s = _oproj_call(o_s.reshape(1, n_seq * tq, d), w_o, y_s, mod_s, ln_g, ln_b)

    kv = lambda a, n, tt: a.reshape(n, tt, ATT_HEADS, 2 * ATT_HD)
    return y_p, y_s, kv(k_p, b, t), kv(v_p, b, t), kv(k_s, n_seq, tq), kv(v_s, n_seq, tq)


def _ssd_layer(j, y_p, y_s, mod_p, mod_s, n_seq, state_ssm, state_conv, ssd_w_in, ssd_conv_w, ssd_conv_b,
               ssd_dt_bias, ssd_a_log, ssd_d, ssd_norm_g, ssd_w_out, ln_g, ln_b):
    b, t, d = y_p.shape
    tq = y_s.shape[1] // n_seq
    w_in = jnp.pad(ssd_w_in[j], ((0, 0), (0, DT_PAD - SSD_HEADS))).astype(BF16)
    w_out = ssd_w_out[j].astype(BF16)
    scan_w = (ssd_conv_w[j].astype(F32), ssd_conv_b[j].astype(F32), ssd_dt_bias[j], ssd_a_log[j], ssd_d[j])

    def mixer(y, mod, seqs, rows, conv0, ssm0):
        z, xbc, dt = _ssd_in_call(y, mod, w_in)
        per_seq = lambda a: a.reshape(seqs, rows, a.shape[-1])
        yy, fin, nconv = _ssd_scan_call(per_seq(xbc), per_seq(dt), conv0, ssm0, *scan_w)
        yy = yy.reshape(y.shape[0], y.shape[1], SSD_D_INNER)
        y_new = _ssd_out_call(yy, z, ssd_norm_g[j].astype(F32), w_out, y, mod, ln_g, ln_b)
        return y_new, fin.reshape(seqs, SSD_HEADS, SSD_HEADDIM, SSD_STATE), nconv

    zero_conv = jnp.zeros((b, SSD_CONV - 1, SSD_CONV_DIM), F32)
    zero_ssm = jnp.zeros((b, SSD_D_INNER, SSD_STATE), F32)
    y_p, ssm_p, conv_p = mixer(y_p, mod_p, b, t, zero_conv, zero_ssm)
    y_s, ssm_s, conv_s = mixer(y_s, mod_s, n_seq, tq, state_conv[j].astype(F32),
                               state_ssm[j].astype(F32).reshape(n_seq, SSD_D_INNER, SSD_STATE))
    return y_p, y_s, ssm_p, conv_p, ssm_s, conv_s


def kernel(x_prompt, x_sample, cache_k, cache_v, state_ssm, state_conv, page_table, c_prompt, c_sample,
           ada_w, ada_b, ln_g, ln_b, rel_bias, attn_w_qkv, attn_lambda, attn_subln_g, attn_w_o,
           ssd_w_in, ssd_conv_w, ssd_conv_b, ssd_dt_bias, ssd_a_log, ssd_d, ssd_norm_g, ssd_w_out,
           ffn_w_in, ffn_w_out):
    b, t, d = x_prompt.shape
    n_seq, tq, _ = x_sample.shape
    depth = ada_w.shape[0]

    n_c = b + n_seq
    rc = -(-n_c // 8) * 8
    c_all = jnp.concatenate([c_prompt, c_sample, jnp.zeros((rc - n_c, d), F32)], axis=0)
    mod_all = _mod_call(c_all, ada_w.reshape(depth * 2, d, 3 * d), ada_b.reshape(depth * 2, 3 * d))

    def mods(i, s):
        m = mod_all[2 * i + s]
        mod_p = m[0:b].reshape(b, 1, 3 * d)
        mod_s = jnp.repeat(m[b:b + n_seq], tq, axis=0).reshape(1, n_seq * tq, 3 * d)
        return mod_p, mod_s

    past = page_table.shape[1] * PAGE_SIZE
    band = _diagonal_band(rel_bias, min(ATTN_NEAR_ROWS, ATTN_TILE, t))
    cfar = _bias_by_distance(rel_bias, jnp.asarray([REL_MAX_DIST], jnp.int32))[:, 0]
    tables = (band, cfar) + _sample_masks(rel_bias, tq, past, PAGES_PER_STEP * PAGE_SIZE)

    y_p = x_prompt
    y_s = x_sample.reshape(1, n_seq * tq, d)
    k_p, v_p, k_s, v_s = [], [], [], []
    ssm_p, conv_p, ssm_s, conv_s = [], [], [], []
    for i in range(depth):
        j = i // 2
        mod_p, mod_s = mods(i, 0)
        if i % 2 == 0:
            lam_init = 0.8 - 0.6 * math.exp(-0.3 * i)
            y_p, y_s, kp, vp, ks, vs = _attention_layer(
                j, lam_init, y_p, y_s, mod_p, mod_s, cache_k, cache_v, page_table, tables,
                attn_w_qkv, attn_lambda, attn_subln_g, attn_w_o, ln_g[i, 0], ln_b[i, 0])
            k_p.append(kp)
            v_p.append(vp)
            k_s.append(ks)
            v_s.append(vs)
        else:
            y_p, y_s, sp, cp, ss, cs = _ssd_layer(
                j, y_p, y_s, mod_p, mod_s, n_seq, state_ssm, state_conv, ssd_w_in, ssd_conv_w, ssd_conv_b,
                ssd_dt_bias, ssd_a_log, ssd_d, ssd_norm_g, ssd_w_out, ln_g[i, 0], ln_b[i, 0])
            ssm_p.append(sp)
            conv_p.append(cp)
            ssm_s.append(ss)
            conv_s.append(cs)
        mod_p, mod_s = mods(i, 1)
        wi = ffn_w_in[i].astype(BF16)
        wo = ffn_w_out[i].astype(BF16)
        y_p = _ffn_call(y_p, mod_p, wi, wo, ln_g[i, 1], ln_b[i, 1])
        y_s = _ffn_call(y_s, mod_s, wi, wo, ln_g[i, 1], ln_b[i, 1])
    return (y_p, y_s.reshape(n_seq, tq, d), jnp.stack(k_p), jnp.stack(v_p), jnp.stack(k_s), jnp.stack(v_s),
            jnp.stack(ssm_p), jnp.stack(conv_p), jnp.stack(ssm_s), jnp.stack(conv_s))
```

```python
import functools
import math

import jax
import jax.numpy as jnp
from jax import lax
from jax.experimental import pallas as pl
from jax.experimental.pallas import tpu as pltpu

F32 = jnp.float32
BF16 = jnp.bfloat16

D_MODEL = 1024
DEPTH = 4
PAGE_SIZE = 128
ATT_HEADS = 8
ATT_HD = 64
ATT_VD = 128
REL_BUCKETS = 32
REL_MAX_DIST = 128
SSD_D_INNER = 2048
SSD_HEADDIM = 64
SSD_HEADS = 32
SSD_GROUPS = 4
SSD_STATE = 128
SSD_CONV = 4
SSD_GN = SSD_GROUPS * SSD_STATE
SSD_CONV_DIM = SSD_D_INNER + 2 * SSD_GN
SSD_CHUNK = 128
FFN_HIDDEN = 2816
DEEPNORM_ALPHA = (2 * DEPTH) ** 0.25
LN_EPS = 1e-5
RMS_EPS = 1e-5

LANES = 128
DT_PAD = LANES
CONV_TAIL = 8
NEG_BIG = -1e30
LOG2E = math.log2(math.e)
VMEM_LIMIT = 56 * 1024 * 1024
ATTN_TILE = 512
ATTN_ROWS = 128
ATTN_NEAR_ROWS = 256
FAR_TILES_PER_STEP = 4
ATTN_HEADS_PER_STEP = 2
BAND_MARGIN = REL_MAX_DIST
PAGES_PER_STEP = 16
FFN_CHUNK = 256


def _cparams(*sem):
    return pltpu.CompilerParams(dimension_semantics=sem, vmem_limit_bytes=VMEM_LIMIT)


def _silu(x):
    return x * jax.nn.sigmoid(x)


def _layer_norm(r, g, b):
    mu = jnp.mean(r, axis=-1, keepdims=True)
    d = r - mu
    var = jnp.mean(d * d, axis=-1, keepdims=True)
    return d * lax.rsqrt(var + LN_EPS) * g + b


def _mod_specs(mod_rows, tm):
    rows = 1 if mod_rows == 1 else tm

    def spec(piece):
        if mod_rows == 1:
            return pl.BlockSpec((None, 1, D_MODEL), lambda b, t: (b, 0, piece))
        return pl.BlockSpec((None, rows, D_MODEL), lambda b, t: (b, t, piece))

    return spec(0), spec(1), spec(2)


def _const_spec(shape):
    n = len(shape)
    return pl.BlockSpec(shape, lambda b, t: (0,) * n)


def _mod_kernel(c_ref, w_ref, b_ref, o_ref):
    a = _silu(c_ref[...]).astype(BF16)
    o_ref[...] = jnp.dot(a, w_ref[...].astype(BF16), preferred_element_type=F32) + b_ref[...]


def _mod_call(c_all, ada_w, ada_b):
    n_sub, d, d3 = ada_w.shape
    rc = c_all.shape[0]
    tn = 1024
    return pl.pallas_call(
        _mod_kernel,
        grid=(n_sub, d3 // tn),
        in_specs=[
            pl.BlockSpec((rc, d), lambda i, j: (0, 0)),
            pl.BlockSpec((None, d, tn), lambda i, j: (i, 0, j)),
            pl.BlockSpec((None, 1, tn), lambda i, j: (i, 0, j)),
        ],
        out_specs=pl.BlockSpec((None, rc, tn), lambda i, j: (i, 0, j)),
        out_shape=jax.ShapeDtypeStruct((n_sub, rc, d3), F32),
        compiler_params=_cparams("arbitrary", "arbitrary"),
        name="adaln_mod",
    )(c_all, ada_w, ada_b.reshape(n_sub, 1, d3))


def _qkv_kernel(x_ref, sh_ref, sc_ref, w_ref, q_ref, k_ref, v_ref, kb_ref, vb_ref):
    d = D_MODEL
    h = (x_ref[...] * (1.0 + sc_ref[...]) + sh_ref[...]).astype(BF16)
    q = jnp.dot(h, w_ref[:, 0:d], preferred_element_type=F32)
    q_ref[...] = (q * (ATT_HD ** -0.5 * LOG2E)).astype(BF16)
    tm = x_ref.shape[0]
    k = jnp.dot(h, w_ref[:, d:2 * d], preferred_element_type=F32)
    kb_ref[...] = k.astype(BF16)
    v = jnp.dot(h, w_ref[:, 2 * d:3 * d], preferred_element_type=F32)
    vb_ref[...] = v.astype(BF16)
    for hd in range(ATT_HEADS):
        hs = slice(hd * LANES, (hd + 1) * LANES)
        k_ref[pl.ds(hd, tm, stride=ATT_HEADS), :] = k[:, hs]
        v_ref[pl.ds(hd, tm, stride=ATT_HEADS), :] = v[:, hs]


def _qkv_call(x, mod, w_bf):
    bx, tx, d = x.shape
    tm = min(512, tx)
    sh, sc, _ = _mod_specs(mod.shape[1], tm)
    tok = pl.BlockSpec((None, tm, d), lambda b, t: (b, t, 0))
    rows = pl.BlockSpec((None, tm * ATT_HEADS, LANES), lambda b, t: (b, t, 0))
    return pl.pallas_call(
        _qkv_kernel,
        grid=(bx, tx // tm),
        in_specs=[tok, sh, sc, _const_spec(w_bf.shape)],
        out_specs=[tok, rows, rows, tok, tok],
        out_shape=[
            jax.ShapeDtypeStruct((bx, tx, d), BF16),
            jax.ShapeDtypeStruct((bx, tx * ATT_HEADS, LANES), F32),
            jax.ShapeDtypeStruct((bx, tx * ATT_HEADS, LANES), F32),
            jax.ShapeDtypeStruct((bx, tx, d), BF16),
            jax.ShapeDtypeStruct((bx, tx, d), BF16),
        ],
        compiler_params=_cparams("arbitrary", "arbitrary"),
        name="qkv_proj",
    )(x, mod, mod, w_bf)


def _stack_maps(q):
    lane = lax.broadcasted_iota(jnp.int32, q.shape, 1)
    zero = jnp.zeros_like(q)
    return jnp.concatenate([jnp.where(lane < ATT_HD, q, zero), jnp.where(lane >= ATT_HD, q, zero)], axis=0)


def _lambda_full(lam_ref, lam_init):
    lp = lam_ref[...]
    a = jnp.sum(lp[0:1, :] * lp[1:2, :], axis=-1, keepdims=True)
    b = jnp.sum(lp[2:3, :] * lp[3:4, :], axis=-1, keepdims=True)
    return jnp.exp(a) - jnp.exp(b) + lam_init


def _diff_finish(num, den, t, lam, g, lam_init):
    o = num[0:t] / den[0:t] - lam * (num[t:2 * t] / den[t:2 * t])
    ms = jnp.mean(o * o, axis=-1, keepdims=True)
    return o * lax.rsqrt(ms + RMS_EPS) * g * (1.0 - lam_init)


def _nt_dot(a, b):
    return lax.dot_general(a, b, (((1,), (1,)), ((), ())), preferred_element_type=F32)


def _lane_tile(x, n):
    return jnp.concatenate([x] * n, axis=1)


def _online_softmax(segments, v_ext, m_prev, acc_prev):
    m_next = m_prev
    for s, c in segments:
        m_curr = jnp.max(s, axis=1, keepdims=True)
        m_next = jnp.maximum(m_next, m_curr if c is None else m_curr + c)
    ps = []
    for s, c in segments:
        shift = m_next if c is None else m_next - c
        ps.append(jnp.exp2(s - _lane_tile(shift, s.shape[1] // LANES)).astype(BF16))
    p = ps[0] if len(ps) == 1 else jnp.concatenate(ps, axis=1)
    alpha = jnp.exp2(m_prev - m_next)
    acc = _lane_tile(alpha, 2) * acc_prev + jnp.dot(p, v_ext, preferred_element_type=F32)
    return m_next, acc


def _attn_prompt_kernel(q_ref, k_ref, v_ref, band_ref, cfar_ref, lam_ref, g_ref, o_ref,
                        qq_sc, vext_sc, m_sc, acc_sc, *, t, lam_init):
    hp = ATTN_HEADS_PER_STEP
    head0 = pl.program_id(1) * hp
    qi = pl.program_id(2)
    rs = min(ATTN_ROWS, t)

    @pl.when(qi == 0)
    def _():
        for hh in range(hp):
            vext_sc[hh, :, 0:ATT_VD] = v_ref[:, hh * LANES:(hh + 1) * LANES]
            vext_sc[hh, :, ATT_VD:2 * ATT_VD] = jnp.ones((vext_sc.shape[1], ATT_VD), BF16)

    for hh in range(hp):
        qq_sc[hh] = _stack_maps(q_ref[:, hh * LANES:(hh + 1) * LANES])
    m_sc[...] = jnp.full(m_sc.shape, NEG_BIG, F32)
    acc_sc[...] = jnp.zeros(acc_sc.shape, F32)

    def chains(rs):
        for hh in range(hp):
            for r in range(2 * t // rs):
                yield hh, slice(hh * LANES, (hh + 1) * LANES), slice(r * rs, (r + 1) * rs), (r * rs) % t

    def tile_update(ki, width):
        start = pl.multiple_of(ki * t, t)
        for hh, hs, rows, _ in chains(rs):
            s = _nt_dot(qq_sc[hh, rows, :], k_ref[pl.ds(start, width), hs])
            m, acc = _online_softmax([(s, cfar_ref[head0 + hh])], vext_sc[hh, pl.ds(start, width), :],
                                     m_sc[hh, rows, :], acc_sc[hh, rows, :])
            m_sc[hh, rows, :] = m
            acc_sc[hh, rows, :] = acc

    def diag_update(base, d0):
        rs = band_ref.shape[1]
        for hh, hs, rows, b0 in chains(rs):
            hi = d0 + b0 + rs
            lo = max(hi - rs - BAND_MARGIN, 0)
            band0 = rs + BAND_MARGIN - (hi - lo)
            q = qq_sc[hh, rows, :]
            near0 = base + lo if isinstance(base, int) else pl.multiple_of(base + lo, LANES)
            s_near = _nt_dot(q, k_ref[pl.ds(near0, hi - lo), hs])
            segments = [(s_near + band_ref[hh, :, band0:band0 + hi - lo], None)]
            if lo > 0:
                segments.insert(0, (_nt_dot(q, k_ref[pl.ds(base, lo), hs]), cfar_ref[head0 + hh]))
            m, acc = _online_softmax(segments, vext_sc[hh, pl.ds(base, hi), :], m_sc[hh, rows, :], acc_sc[hh, rows, :])
            m_sc[hh, rows, :] = m
            acc_sc[hh, rows, :] = acc

    n_far = jnp.maximum(qi - 1, 0)
    n_wide = n_far // FAR_TILES_PER_STEP

    def wide_body(i, carry):
        tile_update(i * FAR_TILES_PER_STEP, FAR_TILES_PER_STEP * t)
        return carry

    lax.fori_loop(0, n_wide, wide_body, 0)

    left = n_far - n_wide * FAR_TILES_PER_STEP
    for r in range(FAR_TILES_PER_STEP):
        @pl.when(jnp.logical_and(qi >= 1, left == r))
        def _(r=r):
            if r:
                tile_update(n_wide * FAR_TILES_PER_STEP, r * t)
            diag_update(pl.multiple_of((qi - 1) * t, t), t)

    @pl.when(qi == 0)
    def _():
        diag_update(0, 0)

    lam = _lambda_full(lam_ref, lam_init)
    for hh in range(hp):
        acc = acc_sc[hh]
        o_ref[:, hh * LANES:(hh + 1) * LANES] = _diff_finish(
            acc[:, 0:ATT_VD], acc[:, ATT_VD:2 * ATT_VD], t, lam, g_ref[...], lam_init).astype(BF16)


def _attn_prompt_call(q_bf, k_bf, v_bf, band, cfar, lam_p, g, lam_init):
    b, tt, d = q_bf.shape
    t = min(ATTN_TILE, tt)
    hp = ATTN_HEADS_PER_STEP
    qtile = pl.BlockSpec((None, t, hp * LANES), lambda b, h, q: (b, q, h))
    heads = pl.BlockSpec((None, tt, hp * LANES), lambda b, h, q: (b, 0, h))
    return pl.pallas_call(
        functools.partial(_attn_prompt_kernel, t=t, lam_init=lam_init),
        grid=(b, ATT_HEADS // hp, tt // t),
        in_specs=[
            qtile, heads, heads,
            pl.BlockSpec((hp,) + band.shape[1:], lambda b, h, q: (h, 0, 0)),
            pl.BlockSpec(memory_space=pltpu.SMEM),
            pl.BlockSpec((4, ATT_HD), lambda b, h, q: (0, 0)),
            pl.BlockSpec((1, ATT_VD), lambda b, h, q: (0, 0)),
        ],
        out_specs=qtile,
        out_shape=jax.ShapeDtypeStruct((b, tt, d), BF16),
        scratch_shapes=[
            pltpu.VMEM((hp, 2 * t, LANES), BF16),
            pltpu.VMEM((hp, tt, 2 * ATT_VD), BF16),
            pltpu.VMEM((hp, 2 * t, LANES), F32),
            pltpu.VMEM((hp, 2 * t, 2 * ATT_VD), F32),
        ],
        compiler_params=_cparams("arbitrary", "arbitrary", "arbitrary"),
        name="attn_prompt",
    )(q_bf, k_bf, v_bf, band, cfar, lam_p, g.reshape(1, ATT_VD))


def _attn_sample_kernel(pt_ref, q_ref, kn_ref, vn_ref, *rest, tq, n_pages, lam_init):
    k_pages = rest[:n_pages]
    v_pages = rest[n_pages:2 * n_pages]
    (mask_far_ref, mask_last_ref, mask_new_ref, lam_ref, g_ref, o_ref,
     qq_sc, m_sc, acc_sc) = rest[2 * n_pages:]
    c = pl.program_id(1)
    last = pl.num_programs(1) - 1

    @pl.when(c == 0)
    def _():
        q = q_ref[...].astype(F32)
        for h in range(ATT_HEADS):
            qq_sc[h * 2 * tq:(h + 1) * 2 * tq, :] = _stack_maps(q[:, h * LANES:(h + 1) * LANES]).astype(BF16)
        m_sc[...] = jnp.full(m_sc.shape, NEG_BIG, F32)
        acc_sc[...] = jnp.zeros(acc_sc.shape, F32)

    def update(k_rows, v_rows, mask):
        ones = jnp.ones((v_rows.shape[0], ATT_VD), BF16)
        v_ext = jnp.concatenate([v_rows.astype(BF16), ones], axis=1)
        s = _nt_dot(qq_sc[...], k_rows.astype(BF16)) + mask
        m, acc = _online_softmax([(s, None)], v_ext, m_sc[...], acc_sc[...])
        m_sc[...] = m
        acc_sc[...] = acc

    @pl.when(c < last)
    def _():
        for j in range(n_pages):
            update(k_pages[j][...], v_pages[j][...], mask_far_ref[...])

    @pl.when(c == last)
    def _():
        for j in range(n_pages):
            mask_ref = mask_last_ref if j == n_pages - 1 else mask_far_ref
            update(k_pages[j][...], v_pages[j][...], mask_ref[...])
        rows_new = kn_ref.shape[0]
        pad = jnp.zeros((LANES - rows_new, LANES), F32)
        update(jnp.concatenate([kn_ref[...], pad], axis=0), jnp.concatenate([vn_ref[...], pad], axis=0),
               mask_new_ref[...])
        lam = _lambda_full(lam_ref, lam_init)
        acc = acc_sc[...]
        for h in range(ATT_HEADS):
            rows = slice(h * 2 * tq, (h + 1) * 2 * tq)
            o_ref[:, h * LANES:(h + 1) * LANES] = _diff_finish(
                acc[rows, 0:ATT_VD], acc[rows, ATT_VD:2 * ATT_VD], tq, lam, g_ref[...], lam_init).astype(BF16)


def _attn_sample_call(q_bf, k_new, v_new, pool_k, pool_v, pages, mask_far, mask_last, mask_new,
                      lam_p, g, lam_init):
    bs, tq, d = q_bf.shape
    n_pages_total = pages.shape[1]
    pp = PAGES_PER_STEP
    n_chunks = n_pages_total // pp
    page_rows = PAGE_SIZE * ATT_HEADS
    n_rows = ATT_HEADS * 2 * tq
    tok = pl.BlockSpec((None, tq, d), lambda b, c, pt: (b, 0, 0))
    new = pl.BlockSpec((None, tq * ATT_HEADS, LANES), lambda b, c, pt: (b, 0, 0))

    def page_spec(j):
        return pl.BlockSpec((None, page_rows, LANES), lambda b, c, pt: (pt[b, c * pp + j], 0, 0))

    def const(shape):
        n = len(shape)
        return pl.BlockSpec(shape, lambda b, c, pt: (0,) * n)

    grid_spec = pltpu.PrefetchScalarGridSpec(
        num_scalar_prefetch=1,
        grid=(bs, n_chunks),
        in_specs=[tok, new, new]
        + [page_spec(j) for j in range(pp)]
        + [page_spec(j) for j in range(pp)]
        + [const(mask_far.shape), const(mask_last.shape), const(mask_new.shape),
           const((4, ATT_HD)), const((1, ATT_VD))],
        out_specs=tok,
        scratch_shapes=[
            pltpu.VMEM((n_rows, LANES), BF16),
            pltpu.VMEM((n_rows, LANES), F32),
            pltpu.VMEM((n_rows, 2 * ATT_VD), F32),
        ],
    )
    return pl.pallas_call(
        functools.partial(_attn_sample_kernel, tq=tq, n_pages=pp, lam_init=lam_init),
        grid_spec=grid_spec,
        out_shape=jax.ShapeDtypeStruct((bs, tq, d), BF16),
        compiler_params=_cparams("arbitrary", "arbitrary"),
        name="attn_sample",
    )(pages, q_bf, k_new, v_new, *([pool_k] * pp), *([pool_v] * pp), mask_far, mask_last, mask_new,
      lam_p, g.reshape(1, ATT_VD))


def _bias_by_distance(rel_bias, n):
    max_exact = REL_BUCKETS // 2
    nn = jnp.maximum(n, 0)
    nf = jnp.maximum(nn, 1).astype(F32)
    large = max_exact + (jnp.log(nf / max_exact) / math.log(REL_MAX_DIST / max_exact)
                         * (REL_BUCKETS - max_exact)).astype(jnp.int32)
    bucket = jnp.where(nn < max_exact, nn, jnp.minimum(large, REL_BUCKETS - 1))
    table = (rel_bias.astype(F32) * LOG2E).T
    onehot = (bucket[None, :] == jnp.arange(REL_BUCKETS)[:, None]).astype(F32)
    vals = jnp.dot(table, onehot, precision=lax.Precision.HIGHEST)
    return jnp.where((n >= 0)[None, :], vals, NEG_BIG)


def _diagonal_band(rel_bias, rows):
    dist = BAND_MARGIN + jnp.arange(rows)[:, None] - jnp.arange(rows + BAND_MARGIN)[None, :]
    return _bias_by_distance(rel_bias, dist.reshape(-1)).reshape(-1, rows, rows + BAND_MARGIN)


def _sample_masks(rel_bias, tq, past):
    h = ATT_HEADS
    same = (jnp.arange(h)[:, None] == jnp.arange(h)[None, :])
    cfar = _bias_by_distance(rel_bias, jnp.asarray([past], jnp.int32))[:, 0]

    def expand(base):
        n_tok = base.shape[-1]
        full = jnp.where(same[:, None, None, None, :], base[:, None, :, :, None], NEG_BIG)
        full = jnp.broadcast_to(full, (h, 2, tq, n_tok, h))
        return full.reshape(h * 2 * tq, n_tok * h)

    i = jnp.arange(tq)
    far = expand(jnp.broadcast_to(cfar[:, None, None], (h, tq, PAGE_SIZE)))
    j = jnp.arange(PAGE_SIZE)
    d_last = (PAGE_SIZE + i[:, None] - j[None, :]).reshape(-1)
    last = expand(_bias_by_distance(rel_bias, d_last).reshape(h, tq, PAGE_SIZE))
    n_new = LANES // h
    jn = jnp.arange(n_new)
    d_new = jnp.where(jn[None, :] < tq, i[:, None] - jn[None, :], -1).reshape(-1)
    new = expand(_bias_by_distance(rel_bias, d_new).reshape(h, tq, n_new))
    return far, last, new


def _oproj_kernel(a_ref, w_ref, x_ref, gate_ref, g_ref, b_ref, o_ref):
    out = jnp.dot(a_ref[...], w_ref[...], preferred_element_type=F32)
    r = DEEPNORM_ALPHA * x_ref[...] + gate_ref[...] * out
    o_ref[...] = _layer_norm(r, g_ref[...], b_ref[...])


def _oproj_call(a_bf, w_bf, x, mod, ln_g, ln_b):
    bx, tx, d = x.shape
    dk = a_bf.shape[-1]
    tm = min(512, tx)
    _, _, gate = _mod_specs(mod.shape[1], tm)
    tok = pl.BlockSpec((None, tm, d), lambda b, t: (b, t, 0))
    return pl.pallas_call(
        _oproj_kernel,
        grid=(bx, tx // tm),
        in_specs=[pl.BlockSpec((None, tm, dk), lambda b, t: (b, t, 0)), _const_spec(w_bf.shape), tok, gate,
                  _const_spec((1, d)), _const_spec((1, d))],
        out_specs=tok,
        out_shape=jax.ShapeDtypeStruct((bx, tx, d), F32),
        compiler_params=_cparams("arbitrary", "arbitrary"),
        name="attn_out_proj",
    )(a_bf, w_bf, x, mod, ln_g.reshape(1, d), ln_b.reshape(1, d))


def _ffn_kernel(x_ref, sh_ref, sc_ref, gate_ref, wi_ref, wo_ref, g_ref, b_ref, o_ref, acc_ref):
    x = x_ref[...]
    h = (x * (1.0 + sc_ref[...]) + sh_ref[...]).astype(BF16)
    for c in range(FFN_HIDDEN // FFN_CHUNK):
        lo = c * FFN_CHUNK
        gte = jnp.dot(h, wi_ref[:, lo:lo + FFN_CHUNK], preferred_element_type=F32)
        up = jnp.dot(h, wi_ref[:, FFN_HIDDEN + lo:FFN_HIDDEN + lo + FFN_CHUNK], preferred_element_type=F32)
        a = (_silu(gte) * up).astype(BF16)
        part = jnp.dot(a, wo_ref[lo:lo + FFN_CHUNK, :], preferred_element_type=F32)
        if c == 0:
            acc_ref[...] = part
        else:
            acc_ref[...] += part
    r = DEEPNORM_ALPHA * x + gate_ref[...] * acc_ref[...]
    o_ref[...] = _layer_norm(r, g_ref[...], b_ref[...])


def _ffn_call(x, mod, wi_bf, wo_bf, ln_g, ln_b):
    bx, tx, d = x.shape
    tm = min(512, tx)
    sh, sc, gate = _mod_specs(mod.shape[1], tm)
    tok = pl.BlockSpec((None, tm, d), lambda b, t: (b, t, 0))
    return pl.pallas_call(
        _ffn_kernel,
        grid=(bx, tx // tm),
        in_specs=[tok, sh, sc, gate, _const_spec(wi_bf.shape), _const_spec(wo_bf.shape),
                  _const_spec((1, d)), _const_spec((1, d))],
        out_specs=tok,
        out_shape=jax.ShapeDtypeStruct((bx, tx, d), F32),
        scratch_shapes=[pltpu.VMEM((tm, d), F32)],
        compiler_params=_cparams("arbitrary", "arbitrary"),
        name="ffn",
    )(x, mod, mod, mod, wi_bf, wo_bf, ln_g.reshape(1, d), ln_b.reshape(1, d))


def _ssd_in_kernel(x_ref, sh_ref, sc_ref, w_ref, z_ref, xbc_ref, dt_ref):
    h = (x_ref[...] * (1.0 + sc_ref[...]) + sh_ref[...]).astype(BF16)
    di = SSD_D_INNER
    for lo in range(0, di, 1024):
        z_ref[:, lo:lo + 1024] = jnp.dot(h, w_ref[:, lo:lo + 1024], preferred_element_type=F32)
    for lo in range(0, SSD_CONV_DIM, 1024):
        xbc_ref[:, lo:lo + 1024] = jnp.dot(h, w_ref[:, di + lo:di + lo + 1024], preferred_element_type=F32)
    lo = di + SSD_CONV_DIM
    dt_ref[...] = jnp.dot(h, w_ref[:, lo:lo + DT_PAD], preferred_element_type=F32)


def _ssd_in_call(x, mod, w_bf):
    bx, tx, d = x.shape
    tm = min(256, tx)
    sh, sc, _ = _mod_specs(mod.shape[1], tm)

    def tok(width):
        return pl.BlockSpec((None, tm, width), lambda b, t: (b, t, 0))

    return pl.pallas_call(
        _ssd_in_kernel,
        grid=(bx, tx // tm),
        in_specs=[tok(d), sh, sc, _const_spec(w_bf.shape)],
        out_specs=[tok(SSD_D_INNER), tok(SSD_CONV_DIM), tok(DT_PAD)],
        out_shape=[
            jax.ShapeDtypeStruct((bx, tx, SSD_D_INNER), F32),
            jax.ShapeDtypeStruct((bx, tx, SSD_CONV_DIM), F32),
            jax.ShapeDtypeStruct((bx, tx, DT_PAD), F32),
        ],
        compiler_params=_cparams("arbitrary", "arbitrary"),
        name="ssd_in_proj",
    )(x, mod, mod, w_bf)


def _ssd_scan_kernel(xbc_ref, dt_ref, conv0_ref, ssm0_ref, cw_ref, cb_ref, dtb_ref, alog_ref, dsk_ref,
                     y_ref, fin_ref, nconv_ref, ext_sc, xs_sc, st_sc, *, rows):
    L = SSD_CHUNK
    di = SSD_D_INNER
    c = pl.program_id(1)
    last = pl.num_programs(1) - 1

    @pl.when(c == 0)
    def _():
        ext_sc[0:CONV_TAIL, :] = jnp.zeros((CONV_TAIL, SSD_CONV_DIM), F32)
        ext_sc[CONV_TAIL - (SSD_CONV - 1):CONV_TAIL, :] = conv0_ref[...]
        st_sc[...] = ssm0_ref[...]

    ext_sc[CONV_TAIL:CONV_TAIL + rows, :] = xbc_ref[...]
    if rows < L:
        ext_sc[CONV_TAIL + rows:CONV_TAIL + L, :] = jnp.zeros((L - rows, SSD_CONV_DIM), F32)

    @pl.when(c == last)
    def _():
        nconv_ref[...] = ext_sc[CONV_TAIL + rows - (SSD_CONV - 1):CONV_TAIL + rows, :]

    cblk = 512
    for lo in range(0, SSD_CONV_DIM, cblk):
        acc = cb_ref[:, lo:lo + cblk] + ext_sc[CONV_TAIL:CONV_TAIL + L, lo:lo + cblk] * cw_ref[3:4, lo:lo + cblk]
        for k in range(1, SSD_CONV):
            w_row = cw_ref[3 - k:4 - k, lo:lo + cblk]
            acc = acc + ext_sc[CONV_TAIL - k:CONV_TAIL - k + L, lo:lo + cblk] * w_row
        xs_sc[:, lo:lo + cblk] = _silu(acc)

    ext_sc[0:CONV_TAIL, :] = ext_sc[L:L + CONV_TAIL, :]

    if rows < L:
        dt_raw = jnp.concatenate([dt_ref[...], jnp.zeros((L - rows, DT_PAD), F32)], axis=0)
    else:
        dt_raw = dt_ref[...]
    xx = dt_raw.T[0:SSD_HEADS, :] + dtb_ref[...]
    dt_t = jnp.maximum(xx, 0.0) + jnp.log1p(jnp.exp(-jnp.abs(xx)))
    if rows < L:
        col = lax.broadcasted_iota(jnp.int32, (SSD_HEADS, L), 1)
        dt_t = jnp.where(col < rows, dt_t, 0.0)
    adt_t = -jnp.exp(alog_ref[...]) * dt_t
    ri = lax.broadcasted_iota(jnp.int32, (L, L), 0)
    ci = lax.broadcasted_iota(jnp.int32, (L, L), 1)
    causal = ci <= ri
    triu = jnp.where(ri <= ci, 1.0, 0.0).astype(F32)
    acs_t = jnp.dot(adt_t, triu, preferred_element_type=F32, precision=lax.Precision.HIGHEST)
    acs = jnp.concatenate([acs_t, jnp.zeros((L - SSD_HEADS, L), F32)], axis=0).T
    acs_end_t = acs_t[:, L - 1:L]
    w_t = dt_t * jnp.exp(acs_end_t - acs_t)
    cdec_t = jnp.exp(acs_end_t)

    half = lax.broadcasted_iota(jnp.int32, (L, LANES), 1) < SSD_HEADDIM
    for g in range(SSD_GROUPS):
        bm = xs_sc[:, di + g * SSD_STATE:di + (g + 1) * SSD_STATE]
        cm = xs_sc[:, di + SSD_GN + g * SSD_STATE:di + SSD_GN + (g + 1) * SSD_STATE]
        bm_bf = bm.astype(BF16)
        cb = _nt_dot(cm.astype(BF16), bm_bf)
        for jj in range(SSD_HEADS // SSD_GROUPS // 2):
            j = g * (SSD_HEADS // SSD_GROUPS // 2) + jj
            ps = slice(j * LANES, (j + 1) * LANES)
            x_pair = xs_sc[:, ps]
            x_bf = x_pair.astype(BF16)
            st_pair = st_sc[ps, :]
            st_bf = st_pair.astype(BF16)
            ys = []
            for hh in range(2):
                hd = 2 * j + hh
                acs_col = jnp.broadcast_to(acs[:, hd:hd + 1], (L, L))
                seg = acs_col - acs_t[hd:hd + 1, :]
                dec = jnp.exp(jnp.where(causal, seg, -jnp.inf))
                mh = (cb * dec * dt_t[hd:hd + 1, :]).astype(BF16)
                y_h = jnp.dot(mh, x_bf, preferred_element_type=F32)
                cme = (cm * jnp.exp(acs_col)).astype(BF16)
                y_h = y_h + _nt_dot(cme, st_bf)
                ys.append(y_h)
            y_pair = jnp.where(half, ys[0], ys[1]) + x_pair * dsk_ref[:, ps]
            y_ref[:, ps] = y_pair[0:rows]
            x_t = x_pair.T
            w_rows = jnp.concatenate([jnp.broadcast_to(w_t[2 * j:2 * j + 1, :], (SSD_HEADDIM, L)),
                                      jnp.broadcast_to(w_t[2 * j + 1:2 * j + 2, :], (SSD_HEADDIM, L))], axis=0)
            d_rows = jnp.concatenate([jnp.broadcast_to(cdec_t[2 * j:2 * j + 1, :], (SSD_HEADDIM, 1)),
                                      jnp.broadcast_to(cdec_t[2 * j + 1:2 * j + 2, :], (SSD_HEADDIM, 1))], axis=0)
            upd = jnp.dot((x_t * w_rows).astype(BF16), bm_bf, preferred_element_type=F32)
            st_sc[ps, :] = st_pair * d_rows + upd

    @pl.when(c == last)
    def _():
        fin_ref[...] = st_sc[...]


def _ssd_scan_call(xbc, dt, conv0, ssm0, conv_w, conv_b, dt_bias, a_log, d_skip):
    bs, tt, _ = xbc.shape
    rows = SSD_CHUNK if tt % SSD_CHUNK == 0 else tt
    nc = tt // rows
    dtb = dt_bias.astype(F32).reshape(SSD_HEADS, 1)
    alog = a_log.astype(F32).reshape(SSD_HEADS, 1)
    dsk = jnp.repeat(d_skip.astype(F32), SSD_HEADDIM).reshape(1, SSD_D_INNER)

    def tok(width):
        return pl.BlockSpec((None, rows, width), lambda b, c: (b, c, 0))

    def per_seq(r, width):
        return pl.BlockSpec((None, r, width), lambda b, c: (b, 0, 0))

    return pl.pallas_call(
        functools.partial(_ssd_scan_kernel, rows=rows),
        grid=(bs, nc),
        in_specs=[tok(SSD_CONV_DIM), tok(DT_PAD), per_seq(SSD_CONV - 1, SSD_CONV_DIM),
                  per_seq(SSD_D_INNER, SSD_STATE),
                  _const_spec((SSD_CONV, SSD_CONV_DIM)), _const_spec((1, SSD_CONV_DIM)),
                  _const_spec((SSD_HEADS, 1)), _const_spec((SSD_HEADS, 1)), _const_spec((1, SSD_D_INNER))],
        out_specs=[tok(SSD_D_INNER), per_seq(SSD_D_INNER, SSD_STATE), per_seq(SSD_CONV - 1, SSD_CONV_DIM)],
        out_shape=[
            jax.ShapeDtypeStruct((bs, tt, SSD_D_INNER), F32),
            jax.ShapeDtypeStruct((bs, SSD_D_INNER, SSD_STATE), F32),
            jax.ShapeDtypeStruct((bs, SSD_CONV - 1, SSD_CONV_DIM), F32),
        ],
        scratch_shapes=[
            pltpu.VMEM((SSD_CHUNK + CONV_TAIL, SSD_CONV_DIM), F32),
            pltpu.VMEM((SSD_CHUNK, SSD_CONV_DIM), F32),
            pltpu.VMEM((SSD_D_INNER, SSD_STATE), F32),
        ],
        compiler_params=_cparams("arbitrary", "arbitrary"),
        name="ssd_scan",
    )(xbc, dt, conv0, ssm0, conv_w, conv_b.reshape(1, SSD_CONV_DIM), dtb, alog, dsk)


def _ssd_out_kernel(y_ref, z_ref, ng_ref, w_ref, x_ref, gate_ref, g_ref, b_ref, o_ref, a_sc):
    gw = SSD_D_INNER // SSD_GROUPS
    for g in range(SSD_GROUPS):
        gs = slice(g * gw, (g + 1) * gw)
        yz = y_ref[:, gs] * _silu(z_ref[:, gs])
        ms = jnp.mean(yz * yz, axis=-1, keepdims=True)
        a_sc[:, gs] = (yz * lax.rsqrt(ms + RMS_EPS) * ng_ref[:, gs]).astype(BF16)
    out = jnp.dot(a_sc[...], w_ref[...], preferred_element_type=F32)
    r = DEEPNORM_ALPHA * x_ref[...] + gate_ref[...] * out
    o_ref[...] = _layer_norm(r, g_ref[...], b_ref[...])


def _ssd_out_call(y, z, norm_g, w_bf, x, mod, ln_g, ln_b):
    bx, tx, d = x.shape
    tm = min(512, tx)
    _, _, gate = _mod_specs(mod.shape[1], tm)

    def tok(width):
        return pl.BlockSpec((None, tm, width), lambda b, t: (b, t, 0))

    return pl.pallas_call(
        _ssd_out_kernel,
        grid=(bx, tx // tm),
        in_specs=[tok(SSD_D_INNER), tok(SSD_D_INNER), _const_spec((1, SSD_D_INNER)), _const_spec(w_bf.shape),
                  tok(d), gate, _const_spec((1, d)), _const_spec((1, d))],
        out_specs=tok(d),
        out_shape=jax.ShapeDtypeStruct((bx, tx, d), F32),
        scratch_shapes=[pltpu.VMEM((tm, SSD_D_INNER), BF16)],
        compiler_params=_cparams("arbitrary", "arbitrary"),
        name="ssd_out_proj",
    )(y, z, norm_g.reshape(1, SSD_D_INNER), w_bf, x, mod, ln_g.reshape(1, d), ln_b.reshape(1, d))


def _attention_layer(j, lam_init, y_p, y_s, mod_p, mod_s, cache_k, cache_v, page_table, tables,
                     attn_w_qkv, attn_lambda, attn_subln_g, attn_w_o, ln_g, ln_b):
    b, t, d = y_p.shape
    band, cfar, mask_far, mask_last, mask_new = tables
    w_qkv = attn_w_qkv[j].astype(BF16)
    w_o = attn_w_o[j].astype(BF16)
    lam_p = attn_lambda[j].astype(F32)
    g = attn_subln_g[j].astype(F32)

    q_bf, k_p, v_p, k_bf, v_bf = _qkv_call(y_p, mod_p, w_qkv)
    o_p = _attn_prompt_call(q_bf, k_bf, v_bf, band, cfar, lam_p, g, lam_init)
    y_p = _oproj_call(o_p, w_o, y_p, mod_p, ln_g, ln_b)

    n_seq = page_table.shape[0]
    tq = y_s.shape[1] // n_seq
    n_pool = cache_k.shape[1]
    page_rows = PAGE_SIZE * ATT_HEADS
    pool_k = cache_k.reshape(cache_k.shape[0] * n_pool, page_rows, LANES)
    pool_v = cache_v.reshape(cache_v.shape[0] * n_pool, page_rows, LANES)
    q_bf, k_s, v_s, _, _ = _qkv_call(y_s, mod_s, w_qkv)
    new_rows = lambda a: a.reshape(n_seq, tq * ATT_HEADS, LANES)
    o_s = _attn_sample_call(q_bf.reshape(n_seq, tq, d), new_rows(k_s), new_rows(v_s), pool_k, pool_v,
                            page_table + j * n_pool, mask_far, mask_last, mask_new, lam_p, g, lam_init)
    y_s = _oproj_call(o_s.reshape(1, n_seq * tq, d), w_o, y_s, mod_s, ln_g, ln_b)

    kv = lambda a, n, tt: a.reshape(n, tt, ATT_HEADS, 2 * ATT_HD)
    return y_p, y_s, kv(k_p, b, t), kv(v_p, b, t), kv(k_s, n_seq, tq), kv(v_s, n_seq, tq)


def _ssd_layer(j, y_p, y_s, mod_p, mod_s, n_seq, state_ssm, state_conv, ssd_w_in, ssd_conv_w, ssd_conv_b,
               ssd_dt_bias, ssd_a_log, ssd_d, ssd_norm_g, ssd_w_out, ln_g, ln_b):
    b, t, d = y_p.shape
    tq = y_s.shape[1] // n_seq
    w_in = jnp.pad(ssd_w_in[j], ((0, 0), (0, DT_PAD - SSD_HEADS))).astype(BF16)
    w_out = ssd_w_out[j].astype(BF16)
    scan_w = (ssd_conv_w[j].astype(F32), ssd_conv_b[j].astype(F32), ssd_dt_bias[j], ssd_a_log[j], ssd_d[j])

    def mixer(y, mod, seqs, rows, conv0, ssm0):
        z, xbc, dt = _ssd_in_call(y, mod, w_in)
        per_seq = lambda a: a.reshape(seqs, rows, a.shape[-1])
        yy, fin, nconv = _ssd_scan_call(per_seq(xbc), per_seq(dt), conv0, ssm0, *scan_w)
        yy = yy.reshape(y.shape[0], y.shape[1], SSD_D_INNER)
        y_new = _ssd_out_call(yy, z, ssd_norm_g[j].astype(F32), w_out, y, mod, ln_g, ln_b)
        return y_new, fin.reshape(seqs, SSD_HEADS, SSD_HEADDIM, SSD_STATE), nconv

    zero_conv = jnp.zeros((b, SSD_CONV - 1, SSD_CONV_DIM), F32)
    zero_ssm = jnp.zeros((b, SSD_D_INNER, SSD_STATE), F32)
    y_p, ssm_p, conv_p = mixer(y_p, mod_p, b, t, zero_conv, zero_ssm)
    y_s, ssm_s, conv_s = mixer(y_s, mod_s, n_seq, tq, state_conv[j].astype(F32),
                               state_ssm[j].astype(F32).reshape(n_seq, SSD_D_INNER, SSD_STATE))
    return y_p, y_s, ssm_p, conv_p, ssm_s, conv_s


def kernel(x_prompt, x_sample, cache_k, cache_v, state_ssm, state_conv, page_table, c_prompt, c_sample,
           ada_w, ada_b, ln_g, ln_b, rel_bias, attn_w_qkv, attn_lambda, attn_subln_g, attn_w_o,
           ssd_w_in, ssd_conv_w, ssd_conv_b, ssd_dt_bias, ssd_a_log, ssd_d, ssd_norm_g, ssd_w_out,
           ffn_w_in, ffn_w_out):
    b, t, d = x_prompt.shape
    n_seq, tq, _ = x_sample.shape
    depth = ada_w.shape[0]

    n_c = b + n_seq
    rc = -(-n_c // 8) * 8
    c_all = jnp.concatenate([c_prompt, c_sample, jnp.zeros((rc - n_c, d), F32)], axis=0)
    mod_all = _mod_call(c_all, ada_w.reshape(depth * 2, d, 3 * d), ada_b.reshape(depth * 2, 3 * d))

    def mods(i, s):
        m = mod_all[2 * i + s]
        mod_p = m[0:b].reshape(b, 1, 3 * d)
        mod_s = jnp.repeat(m[b:b + n_seq], tq, axis=0).reshape(1, n_seq * tq, 3 * d)
        return mod_p, mod_s

    past = page_table.shape[1] * PAGE_SIZE
    band = _diagonal_band(rel_bias, min(ATTN_NEAR_ROWS, ATTN_TILE, t))
    cfar = _bias_by_distance(rel_bias, jnp.asarray([REL_MAX_DIST], jnp.int32))[:, 0]
    tables = (band, cfar) + _sample_masks(rel_bias, tq, past)

    y_p = x_prompt
    y_s = x_sample.reshape(1, n_seq * tq, d)
    k_p, v_p, k_s, v_s = [], [], [], []
    ssm_p, conv_p, ssm_s, conv_s = [], [], [], []
    for i in range(depth):
        j = i // 2
        mod_p, mod_s = mods(i, 0)
        if i % 2 == 0:
            lam_init = 0.8 - 0.6 * math.exp(-0.3 * i)
            y_p, y_s, kp, vp, ks, vs = _attention_layer(
                j, lam_init, y_p, y_s, mod_p, mod_s, cache_k, cache_v, page_table, tables,
                attn_w_qkv, attn_lambda, attn_subln_g, attn_w_o, ln_g[i, 0], ln_b[i, 0])
            k_p.append(kp)
            v_p.append(vp)
            k_s.append(ks)
            v_s.append(vs)
        else:
            y_p, y_s, sp, cp, ss, cs = _ssd_layer(
                j, y_p, y_s, mod_p, mod_s, n_seq, state_ssm, state_conv, ssd_w_in, ssd_conv_w, ssd_conv_b,
                ssd_dt_bias, ssd_a_log, ssd_d, ssd_norm_g, ssd_w_out, ln_g[i, 0], ln_b[i, 0])
            ssm_p.append(sp)
            conv_p.append(cp)
            ssm_s.append(ss)
            conv_s.append(cs)
        mod_p, mod_s = mods(i, 1)
        wi = ffn_w_in[i].astype(BF16)
        wo = ffn_w_out[i].astype(BF16)
        y_p = _ffn_call(y_p, mod_p, wi, wo, ln_g[i, 1], ln_b[i, 1])
        y_s = _ffn_call(y_s, mod_s, wi, wo, ln_g[i, 1], ln_b[i, 1])
    return (y_p, y_s.reshape(n_seq, tq, d), jnp.stack(k_p), jnp.stack(v_p), jnp.stack(k_s), jnp.stack(v_s),
            jnp.stack(ssm_p), jnp.stack(conv_p), jnp.stack(ssm_s), jnp.stack(conv_s))
```

```python
import functools
import math

import jax
import jax.numpy as jnp
from jax import lax
from jax.experimental import pallas as pl
from jax.experimental.pallas import tpu as pltpu

F32 = jnp.float32
BF16 = jnp.bfloat16

D_MODEL = 1024
DEPTH = 4
PAGE_SIZE = 128
ATT_HEADS = 8
ATT_HD = 64
ATT_VD = 128
REL_BUCKETS = 32
REL_MAX_DIST = 128
SSD_D_INNER = 2048
SSD_HEADDIM = 64
SSD_HEADS = 32
SSD_GROUPS = 4
SSD_STATE = 128
SSD_CONV = 4
SSD_GN = SSD_GROUPS * SSD_STATE
SSD_CONV_DIM = SSD_D_INNER + 2 * SSD_GN
SSD_CHUNK = 128
FFN_HIDDEN = 2816
DEEPNORM_ALPHA = (2 * DEPTH) ** 0.25
LN_EPS = 1e-5
RMS_EPS = 1e-5

LANES = 128
DT_PAD = LANES
CONV_TAIL = 8
NEG_BIG = -1e30
LOG2E = math.log2(math.e)
VMEM_LIMIT = 56 * 1024 * 1024
ATTN_TILE = 512
ATTN_ROWS = 128
ATTN_NEAR_ROWS = 256
FAR_TILES_PER_STEP = 4
ATTN_HEADS_PER_STEP = 2
BAND_MARGIN = REL_MAX_DIST
PAGES_PER_STEP = 16
FFN_CHUNK = 256


def _cparams(*sem):
    return pltpu.CompilerParams(dimension_semantics=sem, vmem_limit_bytes=VMEM_LIMIT)


def _silu(x):
    return x * jax.nn.sigmoid(x)


def _layer_norm(r, g, b):
    mu = jnp.mean(r, axis=-1, keepdims=True)
    d = r - mu
    var = jnp.mean(d * d, axis=-1, keepdims=True)
    return d * lax.rsqrt(var + LN_EPS) * g + b


def _mod_specs(mod_rows, tm):
    rows = 1 if mod_rows == 1 else tm

    def spec(piece):
        if mod_rows == 1:
            return pl.BlockSpec((None, 1, D_MODEL), lambda b, t: (b, 0, piece))
        return pl.BlockSpec((None, rows, D_MODEL), lambda b, t: (b, t, piece))

    return spec(0), spec(1), spec(2)


def _const_spec(shape):
    n = len(shape)
    return pl.BlockSpec(shape, lambda b, t: (0,) * n)


def _mod_kernel(c_ref, w_ref, b_ref, o_ref):
    a = _silu(c_ref[...]).astype(BF16)
    o_ref[...] = jnp.dot(a, w_ref[...].astype(BF16), preferred_element_type=F32) + b_ref[...]


def _mod_call(c_all, ada_w, ada_b):
    n_sub, d, d3 = ada_w.shape
    rc = c_all.shape[0]
    tn = 1024
    return pl.pallas_call(
        _mod_kernel,
        grid=(n_sub, d3 // tn),
        in_specs=[
            pl.BlockSpec((rc, d), lambda i, j: (0, 0)),
            pl.BlockSpec((None, d, tn), lambda i, j: (i, 0, j)),
            pl.BlockSpec((None, 1, tn), lambda i, j: (i, 0, j)),
        ],
        out_specs=pl.BlockSpec((None, rc, tn), lambda i, j: (i, 0, j)),
        out_shape=jax.ShapeDtypeStruct((n_sub, rc, d3), F32),
        compiler_params=_cparams("arbitrary", "arbitrary"),
        name="adaln_mod",
    )(c_all, ada_w, ada_b.reshape(n_sub, 1, d3))


def _qkv_kernel(x_ref, sh_ref, sc_ref, w_ref, *rest):
    q_ref, k_ref, v_ref, kb_ref, vb_ref = rest[-5:]
    d = D_MODEL
    h = (x_ref[...] * (1.0 + sc_ref[...]) + sh_ref[...]).astype(BF16)
    q = jnp.dot(h, w_ref[:, 0:d], preferred_element_type=F32)
    q_ref[...] = (q * (ATT_HD ** -0.5 * LOG2E)).astype(BF16)
    tm = x_ref.shape[0]
    k = jnp.dot(h, w_ref[:, d:2 * d], preferred_element_type=F32)
    kb_ref[...] = k.astype(BF16)
    v = jnp.dot(h, w_ref[:, 2 * d:3 * d], preferred_element_type=F32)
    vb_ref[...] = v.astype(BF16)
    for hd in range(ATT_HEADS):
        hs = slice(hd * LANES, (hd + 1) * LANES)
        k_ref[pl.ds(hd, tm, stride=ATT_HEADS), :] = k[:, hs]
        v_ref[pl.ds(hd, tm, stride=ATT_HEADS), :] = v[:, hs]


def _qkv_call(x, mod, w_bf, layer, n_layers, kv_stacks=None):
    bx, tx, d = x.shape
    tm = min(512, tx)
    sh, sc, _ = _mod_specs(mod.shape[1], tm)
    tok = pl.BlockSpec((None, tm, d), lambda b, t: (b, t, 0))
    rows = pl.BlockSpec((None, None, tm * ATT_HEADS, LANES), lambda b, t: (layer, b, t, 0))
    stack = jax.ShapeDtypeStruct((n_layers, bx, tx * ATT_HEADS, LANES), F32)
    in_specs = [tok, sh, sc, _const_spec(w_bf.shape)]
    args = [x, mod, mod, w_bf]
    aliases = {}
    if kv_stacks is not None:
        in_specs += [pl.BlockSpec(memory_space=pl.ANY)] * 2
        aliases = {len(args): 1, len(args) + 1: 2}
        args += list(kv_stacks)
    return pl.pallas_call(
        _qkv_kernel,
        grid=(bx, tx // tm),
        in_specs=in_specs,
        out_specs=[tok, rows, rows, tok, tok],
        out_shape=[jax.ShapeDtypeStruct((bx, tx, d), BF16), stack, stack,
                   jax.ShapeDtypeStruct((bx, tx, d), BF16), jax.ShapeDtypeStruct((bx, tx, d), BF16)],
        input_output_aliases=aliases,
        compiler_params=_cparams("arbitrary", "arbitrary"),
        name="qkv_proj",
    )(*args)


def _stack_maps(q):
    lane = lax.broadcasted_iota(jnp.int32, q.shape, 1)
    zero = jnp.zeros_like(q)
    return jnp.concatenate([jnp.where(lane < ATT_HD, q, zero), jnp.where(lane >= ATT_HD, q, zero)], axis=0)


def _lambda_full(lam_ref, lam_init):
    lp = lam_ref[...]
    a = jnp.sum(lp[0:1, :] * lp[1:2, :], axis=-1, keepdims=True)
    b = jnp.sum(lp[2:3, :] * lp[3:4, :], axis=-1, keepdims=True)
    return jnp.exp(a) - jnp.exp(b) + lam_init


def _diff_finish(num, den, t, lam, g, lam_init):
    o = num[0:t] / den[0:t] - lam * (num[t:2 * t] / den[t:2 * t])
    ms = jnp.mean(o * o, axis=-1, keepdims=True)
    return o * lax.rsqrt(ms + RMS_EPS) * g * (1.0 - lam_init)


def _nt_dot(a, b):
    return lax.dot_general(a, b, (((1,), (1,)), ((), ())), preferred_element_type=F32)


def _lane_tile(x, n):
    return jnp.concatenate([x] * n, axis=1)


def _online_softmax(segments, v_ext, m_prev, acc_prev):
    m_next = m_prev
    for s, c in segments:
        m_curr = jnp.max(s, axis=1, keepdims=True)
        m_next = jnp.maximum(m_next, m_curr if c is None else m_curr + c)
    ps = []
    for s, c in segments:
        shift = m_next if c is None else m_next - c
        ps.append(jnp.exp2(s - _lane_tile(shift, s.shape[1] // LANES)).astype(BF16))
    p = ps[0] if len(ps) == 1 else jnp.concatenate(ps, axis=1)
    alpha = jnp.exp2(m_prev - m_next)
    acc = _lane_tile(alpha, 2) * acc_prev + jnp.dot(p, v_ext, preferred_element_type=F32)
    return m_next, acc


def _attn_prompt_kernel(q_ref, k_ref, v_ref, band_ref, cfar_ref, lam_ref, g_ref, o_ref,
                        qq_sc, vext_sc, m_sc, acc_sc, *, t, lam_init):
    hp = ATTN_HEADS_PER_STEP
    head0 = pl.program_id(1) * hp
    qi = pl.program_id(2)
    rs = min(ATTN_ROWS, t)

    @pl.when(qi == 0)
    def _():
        for hh in range(hp):
            vext_sc[hh, :, 0:ATT_VD] = v_ref[:, hh * LANES:(hh + 1) * LANES]
            vext_sc[hh, :, ATT_VD:2 * ATT_VD] = jnp.ones((vext_sc.shape[1], ATT_VD), BF16)

    for hh in range(hp):
        qq_sc[hh] = _stack_maps(q_ref[:, hh * LANES:(hh + 1) * LANES])
    m_sc[...] = jnp.full(m_sc.shape, NEG_BIG, F32)
    acc_sc[...] = jnp.zeros(acc_sc.shape, F32)

    def chains(rs):
        for hh in range(hp):
            for r in range(2 * t // rs):
                yield hh, slice(hh * LANES, (hh + 1) * LANES), slice(r * rs, (r + 1) * rs), (r * rs) % t

    def tile_update(ki, width):
        start = pl.multiple_of(ki * t, t)
        for hh, hs, rows, _ in chains(rs):
            s = _nt_dot(qq_sc[hh, rows, :], k_ref[pl.ds(start, width), hs])
            m, acc = _online_softmax([(s, cfar_ref[head0 + hh])], vext_sc[hh, pl.ds(start, width), :],
                                     m_sc[hh, rows, :], acc_sc[hh, rows, :])
            m_sc[hh, rows, :] = m
            acc_sc[hh, rows, :] = acc

    def diag_update(base, d0):
        rs = band_ref.shape[1]
        for hh, hs, rows, b0 in chains(rs):
            hi = d0 + b0 + rs
            lo = max(hi - rs - BAND_MARGIN, 0)
            band0 = rs + BAND_MARGIN - (hi - lo)
            q = qq_sc[hh, rows, :]
            near0 = base + lo if isinstance(base, int) else pl.multiple_of(base + lo, LANES)
            s_near = _nt_dot(q, k_ref[pl.ds(near0, hi - lo), hs])
            segments = [(s_near + band_ref[hh, :, band0:band0 + hi - lo], None)]
            if lo > 0:
                segments.insert(0, (_nt_dot(q, k_ref[pl.ds(base, lo), hs]), cfar_ref[head0 + hh]))
            m, acc = _online_softmax(segments, vext_sc[hh, pl.ds(base, hi), :], m_sc[hh, rows, :], acc_sc[hh, rows, :])
            m_sc[hh, rows, :] = m
            acc_sc[hh, rows, :] = acc

    n_far = jnp.maximum(qi - 1, 0)
    n_wide = n_far // FAR_TILES_PER_STEP

    def wide_body(i, carry):
        tile_update(i * FAR_TILES_PER_STEP, FAR_TILES_PER_STEP * t)
        return carry

    lax.fori_loop(0, n_wide, wide_body, 0)

    left = n_far - n_wide * FAR_TILES_PER_STEP
    for r in range(FAR_TILES_PER_STEP):
        @pl.when(jnp.logical_and(qi >= 1, left == r))
        def _(r=r):
            if r:
                tile_update(n_wide * FAR_TILES_PER_STEP, r * t)
            diag_update(pl.multiple_of((qi - 1) * t, t), t)

    @pl.when(qi == 0)
    def _():
        diag_update(0, 0)

    lam = _lambda_full(lam_ref, lam_init)
    for hh in range(hp):
        acc = acc_sc[hh]
        o_ref[:, hh * LANES:(hh + 1) * LANES] = _diff_finish(
            acc[:, 0:ATT_VD], acc[:, ATT_VD:2 * ATT_VD], t, lam, g_ref[...], lam_init).astype(BF16)


def _attn_prompt_call(q_bf, k_bf, v_bf, band, cfar, lam_p, g, lam_init):
    b, tt, d = q_bf.shape
    t = min(ATTN_TILE, tt)
    hp = ATTN_HEADS_PER_STEP
    qtile = pl.BlockSpec((None, t, hp * LANES), lambda b, h, q: (b, q, h))
    heads = pl.BlockSpec((None, tt, hp * LANES), lambda b, h, q: (b, 0, h))
    return pl.pallas_call(
        functools.partial(_attn_prompt_kernel, t=t, lam_init=lam_init),
        grid=(b, ATT_HEADS // hp, tt // t),
        in_specs=[
            qtile, heads, heads,
            pl.BlockSpec((hp,) + band.shape[1:], lambda b, h, q: (h, 0, 0)),
            pl.BlockSpec(memory_space=pltpu.SMEM),
            pl.BlockSpec((4, ATT_HD), lambda b, h, q: (0, 0)),
            pl.BlockSpec((1, ATT_VD), lambda b, h, q: (0, 0)),
        ],
        out_specs=qtile,
        out_shape=jax.ShapeDtypeStruct((b, tt, d), BF16),
        scratch_shapes=[
            pltpu.VMEM((hp, 2 * t, LANES), BF16),
            pltpu.VMEM((hp, tt, 2 * ATT_VD), BF16),
            pltpu.VMEM((hp, 2 * t, LANES), F32),
            pltpu.VMEM((hp, 2 * t, 2 * ATT_VD), F32),
        ],
        compiler_params=_cparams("arbitrary", "arbitrary", "arbitrary"),
        name="attn_prompt",
    )(q_bf, k_bf, v_bf, band, cfar, lam_p, g.reshape(1, ATT_VD))


def _attn_sample_kernel(pt_ref, q_ref, kn_ref, vn_ref, *rest, tq, n_pages, lam_init):
    k_pages = rest[:n_pages]
    v_pages = rest[n_pages:2 * n_pages]
    (mask_far_ref, mask_last_ref, mask_new_ref, lam_ref, g_ref, o_ref,
     qq_sc, m_sc, acc_sc) = rest[2 * n_pages:]
    c = pl.program_id(1)
    last = pl.num_programs(1) - 1

    @pl.when(c == 0)
    def _():
        q = q_ref[...].astype(F32)
        for h in range(ATT_HEADS):
            qq_sc[h * 2 * tq:(h + 1) * 2 * tq, :] = _stack_maps(q[:, h * LANES:(h + 1) * LANES]).astype(BF16)
        m_sc[...] = jnp.full(m_sc.shape, NEG_BIG, F32)
        acc_sc[...] = jnp.zeros(acc_sc.shape, F32)

    def update(k_rows, v_rows, mask):
        ones = jnp.ones((v_rows.shape[0], ATT_VD), BF16)
        v_ext = jnp.concatenate([v_rows.astype(BF16), ones], axis=1)
        s = _nt_dot(qq_sc[...], k_rows.astype(BF16)) + mask
        m, acc = _online_softmax([(s, None)], v_ext, m_sc[...], acc_sc[...])
        m_sc[...] = m
        acc_sc[...] = acc

    @pl.when(c < last)
    def _():
        for j in range(n_pages):
            update(k_pages[j][...], v_pages[j][...], mask_far_ref[...])

    @pl.when(c == last)
    def _():
        for j in range(n_pages):
            mask_ref = mask_last_ref if j == n_pages - 1 else mask_far_ref
            update(k_pages[j][...], v_pages[j][...], mask_ref[...])
        rows_new = kn_ref.shape[0]
        pad = jnp.zeros((LANES - rows_new, LANES), F32)
        update(jnp.concatenate([kn_ref[...], pad], axis=0), jnp.concatenate([vn_ref[...], pad], axis=0),
               mask_new_ref[...])
        lam = _lambda_full(lam_ref, lam_init)
        acc = acc_sc[...]
        for h in range(ATT_HEADS):
            rows = slice(h * 2 * tq, (h + 1) * 2 * tq)
            o_ref[:, h * LANES:(h + 1) * LANES] = _diff_finish(
                acc[rows, 0:ATT_VD], acc[rows, ATT_VD:2 * ATT_VD], tq, lam, g_ref[...], lam_init).astype(BF16)


def _attn_sample_call(q_bf, k_new, v_new, pool_k, pool_v, pages, mask_far, mask_last, mask_new,
                      lam_p, g, lam_init):
    bs, tq, d = q_bf.shape
    n_pages_total = pages.shape[1]
    pp = PAGES_PER_STEP
    n_chunks = n_pages_total // pp
    page_rows = PAGE_SIZE * ATT_HEADS
    n_rows = ATT_HEADS * 2 * tq
    tok = pl.BlockSpec((None, tq, d), lambda b, c, pt: (b, 0, 0))
    new = pl.BlockSpec((None, tq * ATT_HEADS, LANES), lambda b, c, pt: (b, 0, 0))

    def page_spec(j):
        return pl.BlockSpec((None, page_rows, LANES), lambda b, c, pt: (pt[b, c * pp + j], 0, 0))

    def const(shape):
        n = len(shape)
        return pl.BlockSpec(shape, lambda b, c, pt: (0,) * n)

    grid_spec = pltpu.PrefetchScalarGridSpec(
        num_scalar_prefetch=1,
        grid=(bs, n_chunks),
        in_specs=[tok, new, new]
        + [page_spec(j) for j in range(pp)]
        + [page_spec(j) for j in range(pp)]
        + [const(mask_far.shape), const(mask_last.shape), const(mask_new.shape),
           const((4, ATT_HD)), const((1, ATT_VD))],
        out_specs=tok,
        scratch_shapes=[
            pltpu.VMEM((n_rows, LANES), BF16),
            pltpu.VMEM((n_rows, LANES), F32),
            pltpu.VMEM((n_rows, 2 * ATT_VD), F32),
        ],
    )
    return pl.pallas_call(
        functools.partial(_attn_sample_kernel, tq=tq, n_pages=pp, lam_init=lam_init),
        grid_spec=grid_spec,
        out_shape=jax.ShapeDtypeStruct((bs, tq, d), BF16),
        compiler_params=_cparams("arbitrary", "arbitrary"),
        name="attn_sample",
    )(pages, q_bf, k_new, v_new, *([pool_k] * pp), *([pool_v] * pp), mask_far, mask_last, mask_new,
      lam_p, g.reshape(1, ATT_VD))


def _bias_by_distance(rel_bias, n):
    max_exact = REL_BUCKETS // 2
    nn = jnp.maximum(n, 0)
    nf = jnp.maximum(nn, 1).astype(F32)
    large = max_exact + (jnp.log(nf / max_exact) / math.log(REL_MAX_DIST / max_exact)
                         * (REL_BUCKETS - max_exact)).astype(jnp.int32)
    bucket = jnp.where(nn < max_exact, nn, jnp.minimum(large, REL_BUCKETS - 1))
    table = (rel_bias.astype(F32) * LOG2E).T
    onehot = (bucket[None, :] == jnp.arange(REL_BUCKETS)[:, None]).astype(F32)
    vals = jnp.dot(table, onehot, precision=lax.Precision.HIGHEST)
    return jnp.where((n >= 0)[None, :], vals, NEG_BIG)


def _diagonal_band(rel_bias, rows):
    dist = BAND_MARGIN + jnp.arange(rows)[:, None] - jnp.arange(rows + BAND_MARGIN)[None, :]
    return _bias_by_distance(rel_bias, dist.reshape(-1)).reshape(-1, rows, rows + BAND_MARGIN)


def _sample_masks(rel_bias, tq, past):
    h = ATT_HEADS
    same = (jnp.arange(h)[:, None] == jnp.arange(h)[None, :])
    cfar = _bias_by_distance(rel_bias, jnp.asarray([past], jnp.int32))[:, 0]

    def expand(base):
        n_tok = base.shape[-1]
        full = jnp.where(same[:, None, None, None, :], base[:, None, :, :, None], NEG_BIG)
        full = jnp.broadcast_to(full, (h, 2, tq, n_tok, h))
        return full.reshape(h * 2 * tq, n_tok * h)

    i = jnp.arange(tq)
    far = expand(jnp.broadcast_to(cfar[:, None, None], (h, tq, PAGE_SIZE)))
    j = jnp.arange(PAGE_SIZE)
    d_last = (PAGE_SIZE + i[:, None] - j[None, :]).reshape(-1)
    last = expand(_bias_by_distance(rel_bias, d_last).reshape(h, tq, PAGE_SIZE))
    n_new = LANES // h
    jn = jnp.arange(n_new)
    d_new = jnp.where(jn[None, :] < tq, i[:, None] - jn[None, :], -1).reshape(-1)
    new = expand(_bias_by_distance(rel_bias, d_new).reshape(h, tq, n_new))
    return far, last, new


def _oproj_kernel(a_ref, w_ref, x_ref, gate_ref, g_ref, b_ref, o_ref):
    out = jnp.dot(a_ref[...], w_ref[...], preferred_element_type=F32)
    r = DEEPNORM_ALPHA * x_ref[...] + gate_ref[...] * out
    o_ref[...] = _layer_norm(r, g_ref[...], b_ref[...])


def _oproj_call(a_bf, w_bf, x, mod, ln_g, ln_b):
    bx, tx, d = x.shape
    dk = a_bf.shape[-1]
    tm = min(512, tx)
    _, _, gate = _mod_specs(mod.shape[1], tm)
    tok = pl.BlockSpec((None, tm, d), lambda b, t: (b, t, 0))
    return pl.pallas_call(
        _oproj_kernel,
        grid=(bx, tx // tm),
        in_specs=[pl.BlockSpec((None, tm, dk), lambda b, t: (b, t, 0)), _const_spec(w_bf.shape), tok, gate,
                  _const_spec((1, d)), _const_spec((1, d))],
        out_specs=tok,
        out_shape=jax.ShapeDtypeStruct((bx, tx, d), F32),
        compiler_params=_cparams("arbitrary", "arbitrary"),
        name="attn_out_proj",
    )(a_bf, w_bf, x, mod, ln_g.reshape(1, d), ln_b.reshape(1, d))


def _ffn_kernel(x_ref, sh_ref, sc_ref, gate_ref, wi_ref, wo_ref, g_ref, b_ref, o_ref, acc_ref):
    x = x_ref[...]
    h = (x * (1.0 + sc_ref[...]) + sh_ref[...]).astype(BF16)
    for c in range(FFN_HIDDEN // FFN_CHUNK):
        lo = c * FFN_CHUNK
        gte = jnp.dot(h, wi_ref[:, lo:lo + FFN_CHUNK], preferred_element_type=F32)
        up = jnp.dot(h, wi_ref[:, FFN_HIDDEN + lo:FFN_HIDDEN + lo + FFN_CHUNK], preferred_element_type=F32)
        a = (_silu(gte) * up).astype(BF16)
        part = jnp.dot(a, wo_ref[lo:lo + FFN_CHUNK, :], preferred_element_type=F32)
        if c == 0:
            acc_ref[...] = part
        else:
            acc_ref[...] += part
    r = DEEPNORM_ALPHA * x + gate_ref[...] * acc_ref[...]
    o_ref[...] = _layer_norm(r, g_ref[...], b_ref[...])


def _ffn_call(x, mod, wi_bf, wo_bf, ln_g, ln_b):
    bx, tx, d = x.shape
    tm = min(512, tx)
    sh, sc, gate = _mod_specs(mod.shape[1], tm)
    tok = pl.BlockSpec((None, tm, d), lambda b, t: (b, t, 0))
    return pl.pallas_call(
        _ffn_kernel,
        grid=(bx, tx // tm),
        in_specs=[tok, sh, sc, gate, _const_spec(wi_bf.shape), _const_spec(wo_bf.shape),
                  _const_spec((1, d)), _const_spec((1, d))],
        out_specs=tok,
        out_shape=jax.ShapeDtypeStruct((bx, tx, d), F32),
        scratch_shapes=[pltpu.VMEM((tm, d), F32)],
        compiler_params=_cparams("arbitrary", "arbitrary"),
        name="ffn",
    )(x, mod, mod, mod, wi_bf, wo_bf, ln_g.reshape(1, d), ln_b.reshape(1, d))


def _ssd_in_kernel(x_ref, sh_ref, sc_ref, w_ref, z_ref, xbc_ref, dt_ref):
    h = (x_ref[...] * (1.0 + sc_ref[...]) + sh_ref[...]).astype(BF16)
    di = SSD_D_INNER
    for lo in range(0, di, 1024):
        z_ref[:, lo:lo + 1024] = jnp.dot(h, w_ref[:, lo:lo + 1024], preferred_element_type=F32)
    for lo in range(0, SSD_CONV_DIM, 1024):
        xbc_ref[:, lo:lo + 1024] = jnp.dot(h, w_ref[:, di + lo:di + lo + 1024], preferred_element_type=F32)
    lo = di + SSD_CONV_DIM
    dt_ref[...] = jnp.dot(h, w_ref[:, lo:lo + DT_PAD], preferred_element_type=F32)


def _ssd_in_call(x, mod, w_bf):
    bx, tx, d = x.shape
    tm = min(256, tx)
    sh, sc, _ = _mod_specs(mod.shape[1], tm)

    def tok(width):
        return pl.BlockSpec((None, tm, width), lambda b, t: (b, t, 0))

    return pl.pallas_call(
        _ssd_in_kernel,
        grid=(bx, tx // tm),
        in_specs=[tok(d), sh, sc, _const_spec(w_bf.shape)],
        out_specs=[tok(SSD_D_INNER), tok(SSD_CONV_DIM), tok(DT_PAD)],
        out_shape=[
            jax.ShapeDtypeStruct((bx, tx, SSD_D_INNER), F32),
            jax.ShapeDtypeStruct((bx, tx, SSD_CONV_DIM), F32),
            jax.ShapeDtypeStruct((bx, tx, DT_PAD), F32),
        ],
        compiler_params=_cparams("arbitrary", "arbitrary"),
        name="ssd_in_proj",
    )(x, mod, mod, w_bf)


def _ssd_scan_kernel(xbc_ref, dt_ref, conv0_ref, ssm0_ref, cw_ref, cb_ref, dtb_ref, alog_ref, dsk_ref, *rest, rows):
    y_ref, fin_ref, nconv_ref, ext_sc, xs_sc, st_sc = rest[-6:]
    L = SSD_CHUNK
    di = SSD_D_INNER
    c = pl.program_id(1)
    last = pl.num_programs(1) - 1

    @pl.when(c == 0)
    def _():
        ext_sc[0:CONV_TAIL, :] = jnp.zeros((CONV_TAIL, SSD_CONV_DIM), F32)
        ext_sc[CONV_TAIL - (SSD_CONV - 1):CONV_TAIL, :] = conv0_ref[...]
        st_sc[...] = ssm0_ref[...]

    ext_sc[CONV_TAIL:CONV_TAIL + rows, :] = xbc_ref[...]
    if rows < L:
        ext_sc[CONV_TAIL + rows:CONV_TAIL + L, :] = jnp.zeros((L - rows, SSD_CONV_DIM), F32)

    @pl.when(c == last)
    def _():
        nconv_ref[...] = ext_sc[CONV_TAIL + rows - (SSD_CONV - 1):CONV_TAIL + rows, :]

    cblk = 512
    for lo in range(0, SSD_CONV_DIM, cblk):
        acc = cb_ref[:, lo:lo + cblk] + ext_sc[CONV_TAIL:CONV_TAIL + L, lo:lo + cblk] * cw_ref[3:4, lo:lo + cblk]
        for k in range(1, SSD_CONV):
            w_row = cw_ref[3 - k:4 - k, lo:lo + cblk]
            acc = acc + ext_sc[CONV_TAIL - k:CONV_TAIL - k + L, lo:lo + cblk] * w_row
        xs_sc[:, lo:lo + cblk] = _silu(acc)

    ext_sc[0:CONV_TAIL, :] = ext_sc[L:L + CONV_TAIL, :]

    if rows < L:
        dt_raw = jnp.concatenate([dt_ref[...], jnp.zeros((L - rows, DT_PAD), F32)], axis=0)
    else:
        dt_raw = dt_ref[...]
    xx = dt_raw.T[0:SSD_HEADS, :] + dtb_ref[...]
    dt_t = jnp.maximum(xx, 0.0) + jnp.log1p(jnp.exp(-jnp.abs(xx)))
    if rows < L:
        col = lax.broadcasted_iota(jnp.int32, (SSD_HEADS, L), 1)
        dt_t = jnp.where(col < rows, dt_t, 0.0)
    adt_t = -jnp.exp(alog_ref[...]) * dt_t
    ri = lax.broadcasted_iota(jnp.int32, (L, L), 0)
    ci = lax.broadcasted_iota(jnp.int32, (L, L), 1)
    causal = ci <= ri
    triu = jnp.where(ri <= ci, 1.0, 0.0).astype(F32)
    acs_t = jnp.dot(adt_t, triu, preferred_element_type=F32, precision=lax.Precision.HIGHEST)
    acs = jnp.concatenate([acs_t, jnp.zeros((L - SSD_HEADS, L), F32)], axis=0).T
    acs_end_t = acs_t[:, L - 1:L]
    w_t = dt_t * jnp.exp(acs_end_t - acs_t)
    cdec_t = jnp.exp(acs_end_t)

    half = lax.broadcasted_iota(jnp.int32, (L, LANES), 1) < SSD_HEADDIM
    for g in range(SSD_GROUPS):
        bm = xs_sc[:, di + g * SSD_STATE:di + (g + 1) * SSD_STATE]
        cm = xs_sc[:, di + SSD_GN + g * SSD_STATE:di + SSD_GN + (g + 1) * SSD_STATE]
        bm_bf = bm.astype(BF16)
        cb = _nt_dot(cm.astype(BF16), bm_bf)
        for jj in range(SSD_HEADS // SSD_GROUPS // 2):
            j = g * (SSD_HEADS // SSD_GROUPS // 2) + jj
            ps = slice(j * LANES, (j + 1) * LANES)
            x_pair = xs_sc[:, ps]
            x_bf = x_pair.astype(BF16)
            st_pair = st_sc[ps, :]
            st_bf = st_pair.astype(BF16)
            ys = []
            for hh in range(2):
                hd = 2 * j + hh
                acs_col = jnp.broadcast_to(acs[:, hd:hd + 1], (L, L))
                seg = acs_col - acs_t[hd:hd + 1, :]
                dec = jnp.exp(jnp.where(causal, seg, -jnp.inf))
                mh = (cb * dec * dt_t[hd:hd + 1, :]).astype(BF16)
                y_h = jnp.dot(mh, x_bf, preferred_element_type=F32)
                cme = (cm * jnp.exp(acs_col)).astype(BF16)
                y_h = y_h + _nt_dot(cme, st_bf)
                ys.append(y_h)
            y_pair = jnp.where(half, ys[0], ys[1]) + x_pair * dsk_ref[:, ps]
            y_ref[:, ps] = y_pair[0:rows]
            x_t = x_pair.T
            w_rows = jnp.concatenate([jnp.broadcast_to(w_t[2 * j:2 * j + 1, :], (SSD_HEADDIM, L)),
                                      jnp.broadcast_to(w_t[2 * j + 1:2 * j + 2, :], (SSD_HEADDIM, L))], axis=0)
            d_rows = jnp.concatenate([jnp.broadcast_to(cdec_t[2 * j:2 * j + 1, :], (SSD_HEADDIM, 1)),
                                      jnp.broadcast_to(cdec_t[2 * j + 1:2 * j + 2, :], (SSD_HEADDIM, 1))], axis=0)
            upd = jnp.dot((x_t * w_rows).astype(BF16), bm_bf, preferred_element_type=F32)
            st_sc[ps, :] = st_pair * d_rows + upd

    @pl.when(c == last)
    def _():
        fin_ref[...] = st_sc[...]


def _ssd_scan_call(xbc, dt, conv0, ssm0, conv_w, conv_b, dt_bias, a_log, d_skip, layer, n_layers, fin_stack=None):
    bs, tt, _ = xbc.shape
    rows = SSD_CHUNK if tt % SSD_CHUNK == 0 else tt
    nc = tt // rows
    dtb = dt_bias.astype(F32).reshape(SSD_HEADS, 1)
    alog = a_log.astype(F32).reshape(SSD_HEADS, 1)
    dsk = jnp.repeat(d_skip.astype(F32), SSD_HEADDIM).reshape(1, SSD_D_INNER)

    def tok(width):
        return pl.BlockSpec((None, rows, width), lambda b, c: (b, c, 0))

    def per_seq(r, width):
        return pl.BlockSpec((None, r, width), lambda b, c: (b, 0, 0))

    in_specs = [tok(SSD_CONV_DIM), tok(DT_PAD), per_seq(SSD_CONV - 1, SSD_CONV_DIM),
                per_seq(SSD_D_INNER, SSD_STATE),
                _const_spec((SSD_CONV, SSD_CONV_DIM)), _const_spec((1, SSD_CONV_DIM)),
                _const_spec((SSD_HEADS, 1)), _const_spec((SSD_HEADS, 1)), _const_spec((1, SSD_D_INNER))]
    args = [xbc, dt, conv0, ssm0, conv_w, conv_b.reshape(1, SSD_CONV_DIM), dtb, alog, dsk]
    aliases = {}
    if fin_stack is not None:
        in_specs.append(pl.BlockSpec(memory_space=pl.ANY))
        aliases = {len(args): 1}
        args.append(fin_stack)
    fin_spec = pl.BlockSpec((None, None, SSD_D_INNER, SSD_STATE), lambda b, c: (layer, b, 0, 0))
    return pl.pallas_call(
        functools.partial(_ssd_scan_kernel, rows=rows),
        grid=(bs, nc),
        in_specs=in_specs,
        out_specs=[tok(SSD_D_INNER), fin_spec, per_seq(SSD_CONV - 1, SSD_CONV_DIM)],
        out_shape=[
            jax.ShapeDtypeStruct((bs, tt, SSD_D_INNER), F32),
            jax.ShapeDtypeStruct((n_layers, bs, SSD_D_INNER, SSD_STATE), F32),
            jax.ShapeDtypeStruct((bs, SSD_CONV - 1, SSD_CONV_DIM), F32),
        ],
        input_output_aliases=aliases,
        scratch_shapes=[
            pltpu.VMEM((SSD_CHUNK + CONV_TAIL, SSD_CONV_DIM), F32),
            pltpu.VMEM((SSD_CHUNK, SSD_CONV_DIM), F32),
            pltpu.VMEM((SSD_D_INNER, SSD_STATE), F32),
        ],
        compiler_params=_cparams("arbitrary", "arbitrary"),
        name="ssd_scan",
    )(*args)


def _ssd_out_kernel(y_ref, z_ref, ng_ref, w_ref, x_ref, gate_ref, g_ref, b_ref, o_ref, a_sc):
    gw = SSD_D_INNER // SSD_GROUPS
    for g in range(SSD_GROUPS):
        gs = slice(g * gw, (g + 1) * gw)
        yz = y_ref[:, gs] * _silu(z_ref[:, gs])
        ms = jnp.mean(yz * yz, axis=-1, keepdims=True)
        a_sc[:, gs] = (yz * lax.rsqrt(ms + RMS_EPS) * ng_ref[:, gs]).astype(BF16)
    out = jnp.dot(a_sc[...], w_ref[...], preferred_element_type=F32)
    r = DEEPNORM_ALPHA * x_ref[...] + gate_ref[...] * out
    o_ref[...] = _layer_norm(r, g_ref[...], b_ref[...])


def _ssd_out_call(y, z, norm_g, w_bf, x, mod, ln_g, ln_b):
    bx, tx, d = x.shape
    tm = min(512, tx)
    _, _, gate = _mod_specs(mod.shape[1], tm)

    def tok(width):
        return pl.BlockSpec((None, tm, width), lambda b, t: (b, t, 0))

    return pl.pallas_call(
        _ssd_out_kernel,
        grid=(bx, tx // tm),
        in_specs=[tok(SSD_D_INNER), tok(SSD_D_INNER), _const_spec((1, SSD_D_INNER)), _const_spec(w_bf.shape),
                  tok(d), gate, _const_spec((1, d)), _const_spec((1, d))],
        out_specs=tok(d),
        out_shape=jax.ShapeDtypeStruct((bx, tx, d), F32),
        scratch_shapes=[pltpu.VMEM((tm, SSD_D_INNER), BF16)],
        compiler_params=_cparams("arbitrary", "arbitrary"),
        name="ssd_out_proj",
    )(y, z, norm_g.reshape(1, SSD_D_INNER), w_bf, x, mod, ln_g.reshape(1, d), ln_b.reshape(1, d))


def _attention_layer(j, lam_init, y_p, y_s, mod_p, mod_s, cache_k, cache_v, page_table, tables, kv_p, kv_s,
                     attn_w_qkv, attn_lambda, attn_subln_g, attn_w_o, ln_g, ln_b):
    b, t, d = y_p.shape
    band, cfar, mask_far, mask_last, mask_new = tables
    w_qkv = attn_w_qkv[j].astype(BF16)
    w_o = attn_w_o[j].astype(BF16)
    lam_p = attn_lambda[j].astype(F32)
    g = attn_subln_g[j].astype(F32)

    n_layers = attn_w_qkv.shape[0]
    q_bf, k_p, v_p, k_bf, v_bf = _qkv_call(y_p, mod_p, w_qkv, j, n_layers, kv_p)
    o_p = _attn_prompt_call(q_bf, k_bf, v_bf, band, cfar, lam_p, g, lam_init)
    y_p = _oproj_call(o_p, w_o, y_p, mod_p, ln_g, ln_b)

    n_seq = page_table.shape[0]
    tq = y_s.shape[1] // n_seq
    n_pool = cache_k.shape[1]
    page_rows = PAGE_SIZE * ATT_HEADS
    pool_k = cache_k.reshape(cache_k.shape[0] * n_pool, page_rows, LANES)
    pool_v = cache_v.reshape(cache_v.shape[0] * n_pool, page_rows, LANES)
    q_bf, k_s, v_s, _, _ = _qkv_call(y_s, mod_s, w_qkv, j, n_layers, kv_s)
    new_rows = lambda a: a[j].reshape(n_seq, tq * ATT_HEADS, LANES)
    o_s = _attn_sample_call(q_bf.reshape(n_seq, tq, d), new_rows(k_s), new_rows(v_s), pool_k, pool_v,
                            page_table + j * n_pool, mask_far, mask_last, mask_new, lam_p, g, lam_init)
    y_s = _oproj_call(o_s.reshape(1, n_seq * tq, d), w_o, y_s, mod_s, ln_g, ln_b)

    return y_p, y_s, (k_p, v_p), (k_s, v_s)


def _ssd_layer(j, y_p, y_s, mod_p, mod_s, n_seq, fin_p, fin_s, state_ssm, state_conv, ssd_w_in, ssd_conv_w, ssd_conv_b,
               ssd_dt_bias, ssd_a_log, ssd_d, ssd_norm_g, ssd_w_out, ln_g, ln_b):
    b, t, d = y_p.shape
    tq = y_s.shape[1] // n_seq
    w_in = jnp.pad(ssd_w_in[j], ((0, 0), (0, DT_PAD - SSD_HEADS))).astype(BF16)
    w_out = ssd_w_out[j].astype(BF16)
    scan_w = (ssd_conv_w[j].astype(F32), ssd_conv_b[j].astype(F32), ssd_dt_bias[j], ssd_a_log[j], ssd_d[j])

    n_layers = ssd_w_in.shape[0]

    def mixer(y, mod, seqs, rows, conv0, ssm0, fin_stack):
        z, xbc, dt = _ssd_in_call(y, mod, w_in)
        per_seq = lambda a: a.reshape(seqs, rows, a.shape[-1])
        yy, fin, nconv = _ssd_scan_call(per_seq(xbc), per_seq(dt), conv0, ssm0, *scan_w, j, n_layers, fin_stack)
        yy = yy.reshape(y.shape[0], y.shape[1], SSD_D_INNER)
        y_new = _ssd_out_call(yy, z, ssd_norm_g[j].astype(F32), w_out, y, mod, ln_g, ln_b)
        return y_new, fin, nconv

    zero_conv = jnp.zeros((b, SSD_CONV - 1, SSD_CONV_DIM), F32)
    zero_ssm = jnp.zeros((b, SSD_D_INNER, SSD_STATE), F32)
    y_p, ssm_p, conv_p = mixer(y_p, mod_p, b, t, zero_conv, zero_ssm, fin_p)
    y_s, ssm_s, conv_s = mixer(y_s, mod_s, n_seq, tq, state_conv[j].astype(F32),
                               state_ssm[j].astype(F32).reshape(n_seq, SSD_D_INNER, SSD_STATE), fin_s)
    return y_p, y_s, ssm_p, conv_p, ssm_s, conv_s


def kernel(x_prompt, x_sample, cache_k, cache_v, state_ssm, state_conv, page_table, c_prompt, c_sample,
           ada_w, ada_b, ln_g, ln_b, rel_bias, attn_w_qkv, attn_lambda, attn_subln_g, attn_w_o,
           ssd_w_in, ssd_conv_w, ssd_conv_b, ssd_dt_bias, ssd_a_log, ssd_d, ssd_norm_g, ssd_w_out,
           ffn_w_in, ffn_w_out):
    b, t, d = x_prompt.shape
    n_seq, tq, _ = x_sample.shape
    depth = ada_w.shape[0]

    n_c = b + n_seq
    rc = -(-n_c // 8) * 8
    c_all = jnp.concatenate([c_prompt, c_sample, jnp.zeros((rc - n_c, d), F32)], axis=0)
    mod_all = _mod_call(c_all, ada_w.reshape(depth * 2, d, 3 * d), ada_b.reshape(depth * 2, 3 * d))

    def mods(i, s):
        m = mod_all[2 * i + s]
        mod_p = m[0:b].reshape(b, 1, 3 * d)
        mod_s = jnp.repeat(m[b:b + n_seq], tq, axis=0).reshape(1, n_seq * tq, 3 * d)
        return mod_p, mod_s

    past = page_table.shape[1] * PAGE_SIZE
    band = _diagonal_band(rel_bias, min(ATTN_NEAR_ROWS, ATTN_TILE, t))
    cfar = _bias_by_distance(rel_bias, jnp.asarray([REL_MAX_DIST], jnp.int32))[:, 0]
    tables = (band, cfar) + _sample_masks(rel_bias, tq, past)

    y_p = x_prompt
    y_s = x_sample.reshape(1, n_seq * tq, d)
    kv_p = kv_s = None
    fin_p = fin_s = None
    conv_p, conv_s = [], []
    for i in range(depth):
        j = i // 2
        mod_p, mod_s = mods(i, 0)
        if i % 2 == 0:
            lam_init = 0.8 - 0.6 * math.exp(-0.3 * i)
            y_p, y_s, kv_p, kv_s = _attention_layer(
                j, lam_init, y_p, y_s, mod_p, mod_s, cache_k, cache_v, page_table, tables, kv_p, kv_s,
                attn_w_qkv, attn_lambda, attn_subln_g, attn_w_o, ln_g[i, 0], ln_b[i, 0])
        else:
            y_p, y_s, fin_p, cp, fin_s, cs = _ssd_layer(
                j, y_p, y_s, mod_p, mod_s, n_seq, fin_p, fin_s, state_ssm, state_conv, ssd_w_in, ssd_conv_w, ssd_conv_b,
                ssd_dt_bias, ssd_a_log, ssd_d, ssd_norm_g, ssd_w_out, ln_g[i, 0], ln_b[i, 0])
            conv_p.append(cp)
            conv_s.append(cs)
        mod_p, mod_s = mods(i, 1)
        wi = ffn_w_in[i].astype(BF16)
        wo = ffn_w_out[i].astype(BF16)
        y_p = _ffn_call(y_p, mod_p, wi, wo, ln_g[i, 1], ln_b[i, 1])
        y_s = _ffn_call(y_s, mod_s, wi, wo, ln_g[i, 1], ln_b[i, 1])
    n_attn = attn_w_qkv.shape[0]
    kv_out = [a.reshape(n_attn, n, tt, ATT_HEADS, 2 * ATT_HD) for a, n, tt in
              ((kv_p[0], b, t), (kv_p[1], b, t), (kv_s[0], n_seq, tq), (kv_s[1], n_seq, tq))]
    return (y_p, y_s.reshape(n_seq, tq, d), *kv_out,
            fin_p.reshape(fin_p.shape[0], b, SSD_HEADS, SSD_HEADDIM, SSD_STATE), jnp.stack(conv_p),
            fin_s.reshape(fin_s.shape[0], n_seq, SSD_HEADS, SSD_HEADDIM, SSD_STATE), jnp.stack(conv_s))
```

```python
import functools
import math

import jax
import jax.numpy as jnp
from jax import lax
from jax.experimental import pallas as pl
from jax.experimental.pallas import tpu as pltpu

F32 = jnp.float32
BF16 = jnp.bfloat16

D_MODEL = 1024
DEPTH = 4
PAGE_SIZE = 128
ATT_HEADS = 8
ATT_HD = 64
ATT_VD = 128
REL_BUCKETS = 32
REL_MAX_DIST = 128
SSD_D_INNER = 2048
SSD_HEADDIM = 64
SSD_HEADS = 32
SSD_GROUPS = 4
SSD_STATE = 128
SSD_CONV = 4
SSD_GN = SSD_GROUPS * SSD_STATE
SSD_CONV_DIM = SSD_D_INNER + 2 * SSD_GN
SSD_CHUNK = 128
FFN_HIDDEN = 2816
DEEPNORM_ALPHA = (2 * DEPTH) ** 0.25
LN_EPS = 1e-5
RMS_EPS = 1e-5

LANES = 128
DT_PAD = LANES
CONV_TAIL = 8
NEG_BIG = -1e30
LOG2E = math.log2(math.e)
VMEM_LIMIT = 56 * 1024 * 1024
ATTN_TILE = 1024
ATTN_ROWS = 128
ATTN_NEAR_ROWS = 256
FAR_TILES_PER_STEP = 2
ATTN_HEADS_PER_STEP = 2
BAND_MARGIN = REL_MAX_DIST
PAGES_PER_STEP = 16
FFN_CHUNK = 256


def _cparams(*sem):
    return pltpu.CompilerParams(dimension_semantics=sem, vmem_limit_bytes=VMEM_LIMIT)


def _silu(x):
    return x * jax.nn.sigmoid(x)


def _layer_norm(r, g, b):
    mu = jnp.mean(r, axis=-1, keepdims=True)
    d = r - mu
    var = jnp.mean(d * d, axis=-1, keepdims=True)
    return d * lax.rsqrt(var + LN_EPS) * g + b


def _mod_specs(mod_rows, tm):
    rows = 1 if mod_rows == 1 else tm

    def spec(piece):
        if mod_rows == 1:
            return pl.BlockSpec((None, 1, D_MODEL), lambda b, t: (b, 0, piece))
        return pl.BlockSpec((None, rows, D_MODEL), lambda b, t: (b, t, piece))

    return spec(0), spec(1), spec(2)


def _const_spec(shape):
    n = len(shape)
    return pl.BlockSpec(shape, lambda b, t: (0,) * n)


def _mod_kernel(c_ref, w_ref, b_ref, o_ref):
    a = _silu(c_ref[...]).astype(BF16)
    o_ref[...] = jnp.dot(a, w_ref[...].astype(BF16), preferred_element_type=F32) + b_ref[...]


def _mod_call(c_all, ada_w, ada_b):
    n_sub, d, d3 = ada_w.shape
    rc = c_all.shape[0]
    tn = 1024
    return pl.pallas_call(
        _mod_kernel,
        grid=(n_sub, d3 // tn),
        in_specs=[
            pl.BlockSpec((rc, d), lambda i, j: (0, 0)),
            pl.BlockSpec((None, d, tn), lambda i, j: (i, 0, j)),
            pl.BlockSpec((None, 1, tn), lambda i, j: (i, 0, j)),
        ],
        out_specs=pl.BlockSpec((None, rc, tn), lambda i, j: (i, 0, j)),
        out_shape=jax.ShapeDtypeStruct((n_sub, rc, d3), F32),
        compiler_params=_cparams("arbitrary", "arbitrary"),
        name="adaln_mod",
    )(c_all, ada_w, ada_b.reshape(n_sub, 1, d3))


def _qkv_kernel(x_ref, sh_ref, sc_ref, w_ref, *rest):
    q_ref, k_ref, v_ref, kb_ref, vb_ref = rest[-5:]
    d = D_MODEL
    h = (x_ref[...] * (1.0 + sc_ref[...]) + sh_ref[...]).astype(BF16)
    q = jnp.dot(h, w_ref[:, 0:d], preferred_element_type=F32)
    q_ref[...] = (q * (ATT_HD ** -0.5 * LOG2E)).astype(BF16)
    tm = x_ref.shape[0]
    k = jnp.dot(h, w_ref[:, d:2 * d], preferred_element_type=F32)
    kb_ref[...] = k.astype(BF16)
    v = jnp.dot(h, w_ref[:, 2 * d:3 * d], preferred_element_type=F32)
    vb_ref[...] = v.astype(BF16)
    for hd in range(ATT_HEADS):
        hs = slice(hd * LANES, (hd + 1) * LANES)
        k_ref[pl.ds(hd, tm, stride=ATT_HEADS), :] = k[:, hs]
        v_ref[pl.ds(hd, tm, stride=ATT_HEADS), :] = v[:, hs]


def _qkv_call(x, mod, w_bf, layer, n_layers, kv_stacks=None):
    bx, tx, d = x.shape
    tm = min(512, tx)
    sh, sc, _ = _mod_specs(mod.shape[1], tm)
    tok = pl.BlockSpec((None, tm, d), lambda b, t: (b, t, 0))
    rows = pl.BlockSpec((None, None, tm * ATT_HEADS, LANES), lambda b, t: (layer, b, t, 0))
    stack = jax.ShapeDtypeStruct((n_layers, bx, tx * ATT_HEADS, LANES), F32)
    in_specs = [tok, sh, sc, _const_spec(w_bf.shape)]
    args = [x, mod, mod, w_bf]
    aliases = {}
    if kv_stacks is not None:
        in_specs += [pl.BlockSpec(memory_space=pl.ANY)] * 2
        aliases = {len(args): 1, len(args) + 1: 2}
        args += list(kv_stacks)
    return pl.pallas_call(
        _qkv_kernel,
        grid=(bx, tx // tm),
        in_specs=in_specs,
        out_specs=[tok, rows, rows, tok, tok],
        out_shape=[jax.ShapeDtypeStruct((bx, tx, d), BF16), stack, stack,
                   jax.ShapeDtypeStruct((bx, tx, d), BF16), jax.ShapeDtypeStruct((bx, tx, d), BF16)],
        input_output_aliases=aliases,
        compiler_params=_cparams("arbitrary", "arbitrary"),
        name="qkv_proj",
    )(*args)


def _stack_maps(q):
    lane = lax.broadcasted_iota(jnp.int32, q.shape, 1)
    zero = jnp.zeros_like(q)
    return jnp.concatenate([jnp.where(lane < ATT_HD, q, zero), jnp.where(lane >= ATT_HD, q, zero)], axis=0)


def _lambda_full(lam_ref, lam_init):
    lp = lam_ref[...]
    a = jnp.sum(lp[0:1, :] * lp[1:2, :], axis=-1, keepdims=True)
    b = jnp.sum(lp[2:3, :] * lp[3:4, :], axis=-1, keepdims=True)
    return jnp.exp(a) - jnp.exp(b) + lam_init


def _diff_finish(num, den, t, lam, g, lam_init):
    o = num[0:t] / den[0:t] - lam * (num[t:2 * t] / den[t:2 * t])
    ms = jnp.mean(o * o, axis=-1, keepdims=True)
    return o * lax.rsqrt(ms + RMS_EPS) * g * (1.0 - lam_init)


def _nt_dot(a, b):
    return lax.dot_general(a, b, (((1,), (1,)), ((), ())), preferred_element_type=F32)


def _lane_tile(x, n):
    return jnp.concatenate([x] * n, axis=1)


def _online_softmax(segments, v_ext, m_prev, acc_prev):
    m_next = m_prev
    for s, c in segments:
        m_curr = jnp.max(s, axis=1, keepdims=True)
        m_next = jnp.maximum(m_next, m_curr if c is None else m_curr + c)
    ps = []
    for s, c in segments:
        shift = m_next if c is None else m_next - c
        ps.append(jnp.exp2(s - _lane_tile(shift, s.shape[1] // LANES)).astype(BF16))
    p = ps[0] if len(ps) == 1 else jnp.concatenate(ps, axis=1)
    alpha = jnp.exp2(m_prev - m_next)
    acc = _lane_tile(alpha, 2) * acc_prev + jnp.dot(p, v_ext, preferred_element_type=F32)
    return m_next, acc


def _attn_prompt_kernel(q_ref, k_ref, v_ref, band_ref, cfar_ref, lam_ref, g_ref, o_ref,
                        qq_sc, vext_sc, m_sc, acc_sc, *, t, lam_init):
    hp = ATTN_HEADS_PER_STEP
    head0 = pl.program_id(1) * hp
    qi = pl.program_id(2)
    rs = min(ATTN_ROWS, t)

    @pl.when(qi == 0)
    def _():
        for hh in range(hp):
            vext_sc[hh, :, 0:ATT_VD] = v_ref[:, hh * LANES:(hh + 1) * LANES]
            vext_sc[hh, :, ATT_VD:2 * ATT_VD] = jnp.ones((vext_sc.shape[1], ATT_VD), BF16)

    for hh in range(hp):
        qq_sc[hh] = _stack_maps(q_ref[:, hh * LANES:(hh + 1) * LANES])
    m_sc[...] = jnp.full(m_sc.shape, NEG_BIG, F32)
    acc_sc[...] = jnp.zeros(acc_sc.shape, F32)

    def chains(rs):
        for hh in range(hp):
            for r in range(2 * t // rs):
                yield hh, slice(hh * LANES, (hh + 1) * LANES), slice(r * rs, (r + 1) * rs), (r * rs) % t

    def tile_update(ki, width):
        start = pl.multiple_of(ki * t, t)
        for hh, hs, rows, _ in chains(rs):
            s = _nt_dot(qq_sc[hh, rows, :], k_ref[pl.ds(start, width), hs])
            m, acc = _online_softmax([(s, cfar_ref[head0 + hh])], vext_sc[hh, pl.ds(start, width), :],
                                     m_sc[hh, rows, :], acc_sc[hh, rows, :])
            m_sc[hh, rows, :] = m
            acc_sc[hh, rows, :] = acc

    def diag_update(base, d0):
        rs = band_ref.shape[1]
        for hh, hs, rows, b0 in chains(rs):
            hi = d0 + b0 + rs
            lo = max(hi - rs - BAND_MARGIN, 0)
            band0 = rs + BAND_MARGIN - (hi - lo)
            q = qq_sc[hh, rows, :]
            near0 = base + lo if isinstance(base, int) else pl.multiple_of(base + lo, LANES)
            s_near = _nt_dot(q, k_ref[pl.ds(near0, hi - lo), hs])
            segments = [(s_near + band_ref[hh, :, band0:band0 + hi - lo], None)]
            if lo > 0:
                segments.insert(0, (_nt_dot(q, k_ref[pl.ds(base, lo), hs]), cfar_ref[head0 + hh]))
            m, acc = _online_softmax(segments, vext_sc[hh, pl.ds(base, hi), :], m_sc[hh, rows, :], acc_sc[hh, rows, :])
            m_sc[hh, rows, :] = m
            acc_sc[hh, rows, :] = acc

    n_far = jnp.maximum(qi - 1, 0)
    n_wide = n_far // FAR_TILES_PER_STEP

    def wide_body(i, carry):
        tile_update(i * FAR_TILES_PER_STEP, FAR_TILES_PER_STEP * t)
        return carry

    lax.fori_loop(0, n_wide, wide_body, 0)

    left = n_far - n_wide * FAR_TILES_PER_STEP
    for r in range(FAR_TILES_PER_STEP):
        @pl.when(jnp.logical_and(qi >= 1, left == r))
        def _(r=r):
            if r:
                tile_update(n_wide * FAR_TILES_PER_STEP, r * t)
            diag_update(pl.multiple_of((qi - 1) * t, t), t)

    @pl.when(qi == 0)
    def _():
        diag_update(0, 0)

    lam = _lambda_full(lam_ref, lam_init)
    for hh in range(hp):
        acc = acc_sc[hh]
        o_ref[:, hh * LANES:(hh + 1) * LANES] = _diff_finish(
            acc[:, 0:ATT_VD], acc[:, ATT_VD:2 * ATT_VD], t, lam, g_ref[...], lam_init).astype(BF16)


def _attn_prompt_call(q_bf, k_bf, v_bf, band, cfar, lam_p, g, lam_init):
    b, tt, d = q_bf.shape
    t = min(ATTN_TILE, tt)
    hp = ATTN_HEADS_PER_STEP
    qtile = pl.BlockSpec((None, t, hp * LANES), lambda b, h, q: (b, q, h))
    heads = pl.BlockSpec((None, tt, hp * LANES), lambda b, h, q: (b, 0, h))
    return pl.pallas_call(
        functools.partial(_attn_prompt_kernel, t=t, lam_init=lam_init),
        grid=(b, ATT_HEADS // hp, tt // t),
        in_specs=[
            qtile, heads, heads,
            pl.BlockSpec((hp,) + band.shape[1:], lambda b, h, q: (h, 0, 0)),
            pl.BlockSpec(memory_space=pltpu.SMEM),
            pl.BlockSpec((4, ATT_HD), lambda b, h, q: (0, 0)),
            pl.BlockSpec((1, ATT_VD), lambda b, h, q: (0, 0)),
        ],
        out_specs=qtile,
        out_shape=jax.ShapeDtypeStruct((b, tt, d), BF16),
        scratch_shapes=[
            pltpu.VMEM((hp, 2 * t, LANES), BF16),
            pltpu.VMEM((hp, tt, 2 * ATT_VD), BF16),
            pltpu.VMEM((hp, 2 * t, LANES), F32),
            pltpu.VMEM((hp, 2 * t, 2 * ATT_VD), F32),
        ],
        compiler_params=_cparams("arbitrary", "arbitrary", "arbitrary"),
        name="attn_prompt",
    )(q_bf, k_bf, v_bf, band, cfar, lam_p, g.reshape(1, ATT_VD))


def _attn_sample_kernel(pt_ref, q_ref, kn_ref, vn_ref, *rest, tq, n_pages, lam_init):
    k_pages = rest[:n_pages]
    v_pages = rest[n_pages:2 * n_pages]
    (mask_far_ref, mask_last_ref, mask_new_ref, lam_ref, g_ref, o_ref,
     qq_sc, m_sc, acc_sc) = rest[2 * n_pages:]
    c = pl.program_id(1)
    last = pl.num_programs(1) - 1

    @pl.when(c == 0)
    def _():
        q = q_ref[...].astype(F32)
        for h in range(ATT_HEADS):
            qq_sc[h * 2 * tq:(h + 1) * 2 * tq, :] = _stack_maps(q[:, h * LANES:(h + 1) * LANES]).astype(BF16)
        m_sc[...] = jnp.full(m_sc.shape, NEG_BIG, F32)
        acc_sc[...] = jnp.zeros(acc_sc.shape, F32)

    def update(k_rows, v_rows, mask):
        ones = jnp.ones((v_rows.shape[0], ATT_VD), BF16)
        v_ext = jnp.concatenate([v_rows.astype(BF16), ones], axis=1)
        s = _nt_dot(qq_sc[...], k_rows.astype(BF16)) + mask
        m, acc = _online_softmax([(s, None)], v_ext, m_sc[...], acc_sc[...])
        m_sc[...] = m
        acc_sc[...] = acc

    @pl.when(c < last)
    def _():
        for j in range(n_pages):
            update(k_pages[j][...], v_pages[j][...], mask_far_ref[...])

    @pl.when(c == last)
    def _():
        for j in range(n_pages):
            mask_ref = mask_last_ref if j == n_pages - 1 else mask_far_ref
            update(k_pages[j][...], v_pages[j][...], mask_ref[...])
        rows_new = kn_ref.shape[0]
        pad = jnp.zeros((LANES - rows_new, LANES), F32)
        update(jnp.concatenate([kn_ref[...], pad], axis=0), jnp.concatenate([vn_ref[...], pad], axis=0),
               mask_new_ref[...])
        lam = _lambda_full(lam_ref, lam_init)
        acc = acc_sc[...]
        for h in range(ATT_HEADS):
            rows = slice(h * 2 * tq, (h + 1) * 2 * tq)
            o_ref[:, h * LANES:(h + 1) * LANES] = _diff_finish(
                acc[rows, 0:ATT_VD], acc[rows, ATT_VD:2 * ATT_VD], tq, lam, g_ref[...], lam_init).astype(BF16)


def _attn_sample_call(q_bf, k_new, v_new, pool_k, pool_v, pages, mask_far, mask_last, mask_new,
                      lam_p, g, lam_init):
    bs, tq, d = q_bf.shape
    n_pages_total = pages.shape[1]
    pp = PAGES_PER_STEP
    n_chunks = n_pages_total // pp
    page_rows = PAGE_SIZE * ATT_HEADS
    n_rows = ATT_HEADS * 2 * tq
    tok = pl.BlockSpec((None, tq, d), lambda b, c, pt: (b, 0, 0))
    new = pl.BlockSpec((None, tq * ATT_HEADS, LANES), lambda b, c, pt: (b, 0, 0))

    def page_spec(j):
        return pl.BlockSpec((None, page_rows, LANES), lambda b, c, pt: (pt[b, c * pp + j], 0, 0))

    def const(shape):
        n = len(shape)
        return pl.BlockSpec(shape, lambda b, c, pt: (0,) * n)

    grid_spec = pltpu.PrefetchScalarGridSpec(
        num_scalar_prefetch=1,
        grid=(bs, n_chunks),
        in_specs=[tok, new, new]
        + [page_spec(j) for j in range(pp)]
        + [page_spec(j) for j in range(pp)]
        + [const(mask_far.shape), const(mask_last.shape), const(mask_new.shape),
           const((4, ATT_HD)), const((1, ATT_VD))],
        out_specs=tok,
        scratch_shapes=[
            pltpu.VMEM((n_rows, LANES), BF16),
            pltpu.VMEM((n_rows, LANES), F32),
            pltpu.VMEM((n_rows, 2 * ATT_VD), F32),
        ],
    )
    return pl.pallas_call(
        functools.partial(_attn_sample_kernel, tq=tq, n_pages=pp, lam_init=lam_init),
        grid_spec=grid_spec,
        out_shape=jax.ShapeDtypeStruct((bs, tq, d), BF16),
        compiler_params=_cparams("arbitrary", "arbitrary"),
        name="attn_sample",
    )(pages, q_bf, k_new, v_new, *([pool_k] * pp), *([pool_v] * pp), mask_far, mask_last, mask_new,
      lam_p, g.reshape(1, ATT_VD))


def _bias_by_distance(rel_bias, n):
    max_exact = REL_BUCKETS // 2
    nn = jnp.maximum(n, 0)
    nf = jnp.maximum(nn, 1).astype(F32)
    large = max_exact + (jnp.log(nf / max_exact) / math.log(REL_MAX_DIST / max_exact)
                         * (REL_BUCKETS - max_exact)).astype(jnp.int32)
    bucket = jnp.where(nn < max_exact, nn, jnp.minimum(large, REL_BUCKETS - 1))
    table = (rel_bias.astype(F32) * LOG2E).T
    onehot = (bucket[None, :] == jnp.arange(REL_BUCKETS)[:, None]).astype(F32)
    vals = jnp.dot(table, onehot, precision=lax.Precision.HIGHEST)
    return jnp.where((n >= 0)[None, :], vals, NEG_BIG)


def _diagonal_band(rel_bias, rows):
    dist = BAND_MARGIN + jnp.arange(rows)[:, None] - jnp.arange(rows + BAND_MARGIN)[None, :]
    return _bias_by_distance(rel_bias, dist.reshape(-1)).reshape(-1, rows, rows + BAND_MARGIN)


def _sample_masks(rel_bias, tq, past):
    h = ATT_HEADS
    same = (jnp.arange(h)[:, None] == jnp.arange(h)[None, :])
    cfar = _bias_by_distance(rel_bias, jnp.asarray([past], jnp.int32))[:, 0]

    def expand(base):
        n_tok = base.shape[-1]
        full = jnp.where(same[:, None, None, None, :], base[:, None, :, :, None], NEG_BIG)
        full = jnp.broadcast_to(full, (h, 2, tq, n_tok, h))
        return full.reshape(h * 2 * tq, n_tok * h)

    i = jnp.arange(tq)
    far = expand(jnp.broadcast_to(cfar[:, None, None], (h, tq, PAGE_SIZE)))
    j = jnp.arange(PAGE_SIZE)
    d_last = (PAGE_SIZE + i[:, None] - j[None, :]).reshape(-1)
    last = expand(_bias_by_distance(rel_bias, d_last).reshape(h, tq, PAGE_SIZE))
    n_new = LANES // h
    jn = jnp.arange(n_new)
    d_new = jnp.where(jn[None, :] < tq, i[:, None] - jn[None, :], -1).reshape(-1)
    new = expand(_bias_by_distance(rel_bias, d_new).reshape(h, tq, n_new))
    return far, last, new


def _oproj_kernel(a_ref, w_ref, x_ref, gate_ref, g_ref, b_ref, o_ref):
    out = jnp.dot(a_ref[...], w_ref[...], preferred_element_type=F32)
    r = DEEPNORM_ALPHA * x_ref[...] + gate_ref[...] * out
    o_ref[...] = _layer_norm(r, g_ref[...], b_ref[...])


def _oproj_call(a_bf, w_bf, x, mod, ln_g, ln_b):
    bx, tx, d = x.shape
    dk = a_bf.shape[-1]
    tm = min(512, tx)
    _, _, gate = _mod_specs(mod.shape[1], tm)
    tok = pl.BlockSpec((None, tm, d), lambda b, t: (b, t, 0))
    return pl.pallas_call(
        _oproj_kernel,
        grid=(bx, tx // tm),
        in_specs=[pl.BlockSpec((None, tm, dk), lambda b, t: (b, t, 0)), _const_spec(w_bf.shape), tok, gate,
                  _const_spec((1, d)), _const_spec((1, d))],
        out_specs=tok,
        out_shape=jax.ShapeDtypeStruct((bx, tx, d), F32),
        compiler_params=_cparams("arbitrary", "arbitrary"),
        name="attn_out_proj",
    )(a_bf, w_bf, x, mod, ln_g.reshape(1, d), ln_b.reshape(1, d))


def _ffn_kernel(x_ref, sh_ref, sc_ref, gate_ref, wi_ref, wo_ref, g_ref, b_ref, o_ref, acc_ref):
    x = x_ref[...]
    h = (x * (1.0 + sc_ref[...]) + sh_ref[...]).astype(BF16)
    for c in range(FFN_HIDDEN // FFN_CHUNK):
        lo = c * FFN_CHUNK
        gte = jnp.dot(h, wi_ref[:, lo:lo + FFN_CHUNK], preferred_element_type=F32)
        up = jnp.dot(h, wi_ref[:, FFN_HIDDEN + lo:FFN_HIDDEN + lo + FFN_CHUNK], preferred_element_type=F32)
        a = (_silu(gte) * up).astype(BF16)
        part = jnp.dot(a, wo_ref[lo:lo + FFN_CHUNK, :], preferred_element_type=F32)
        if c == 0:
            acc_ref[...] = part
        else:
            acc_ref[...] += part
    r = DEEPNORM_ALPHA * x + gate_ref[...] * acc_ref[...]
    o_ref[...] = _layer_norm(r, g_ref[...], b_ref[...])


def _ffn_call(x, mod, wi_bf, wo_bf, ln_g, ln_b):
    bx, tx, d = x.shape
    tm = min(512, tx)
    sh, sc, gate = _mod_specs(mod.shape[1], tm)
    tok = pl.BlockSpec((None, tm, d), lambda b, t: (b, t, 0))
    return pl.pallas_call(
        _ffn_kernel,
        grid=(bx, tx // tm),
        in_specs=[tok, sh, sc, gate, _const_spec(wi_bf.shape), _const_spec(wo_bf.shape),
                  _const_spec((1, d)), _const_spec((1, d))],
        out_specs=tok,
        out_shape=jax.ShapeDtypeStruct((bx, tx, d), F32),
        scratch_shapes=[pltpu.VMEM((tm, d), F32)],
        compiler_params=_cparams("arbitrary", "arbitrary"),
        name="ffn",
    )(x, mod, mod, mod, wi_bf, wo_bf, ln_g.reshape(1, d), ln_b.reshape(1, d))


def _ssd_in_kernel(x_ref, sh_ref, sc_ref, w_ref, z_ref, xbc_ref, dt_ref):
    h = (x_ref[...] * (1.0 + sc_ref[...]) + sh_ref[...]).astype(BF16)
    di = SSD_D_INNER
    for lo in range(0, di, 1024):
        z_ref[:, lo:lo + 1024] = jnp.dot(h, w_ref[:, lo:lo + 1024], preferred_element_type=F32)
    for lo in range(0, SSD_CONV_DIM, 1024):
        xbc_ref[:, lo:lo + 1024] = jnp.dot(h, w_ref[:, di + lo:di + lo + 1024], preferred_element_type=F32)
    lo = di + SSD_CONV_DIM
    dt_ref[...] = jnp.dot(h, w_ref[:, lo:lo + DT_PAD], preferred_element_type=F32)


def _ssd_in_call(x, mod, w_bf):
    bx, tx, d = x.shape
    tm = min(256, tx)
    sh, sc, _ = _mod_specs(mod.shape[1], tm)

    def tok(width):
        return pl.BlockSpec((None, tm, width), lambda b, t: (b, t, 0))

    return pl.pallas_call(
        _ssd_in_kernel,
        grid=(bx, tx // tm),
        in_specs=[tok(d), sh, sc, _const_spec(w_bf.shape)],
        out_specs=[tok(SSD_D_INNER), tok(SSD_CONV_DIM), tok(DT_PAD)],
        out_shape=[
            jax.ShapeDtypeStruct((bx, tx, SSD_D_INNER), F32),
            jax.ShapeDtypeStruct((bx, tx, SSD_CONV_DIM), F32),
            jax.ShapeDtypeStruct((bx, tx, DT_PAD), F32),
        ],
        compiler_params=_cparams("arbitrary", "arbitrary"),
        name="ssd_in_proj",
    )(x, mod, mod, w_bf)


def _ssd_scan_kernel(xbc_ref, dt_ref, conv0_ref, ssm0_ref, cw_ref, cb_ref, dtb_ref, alog_ref, dsk_ref, *rest, rows):
    y_ref, fin_ref, nconv_ref, ext_sc, xs_sc, st_sc = rest[-6:]
    L = SSD_CHUNK
    di = SSD_D_INNER
    c = pl.program_id(1)
    last = pl.num_programs(1) - 1

    @pl.when(c == 0)
    def _():
        ext_sc[0:CONV_TAIL, :] = jnp.zeros((CONV_TAIL, SSD_CONV_DIM), F32)
        ext_sc[CONV_TAIL - (SSD_CONV - 1):CONV_TAIL, :] = conv0_ref[...]
        st_sc[...] = ssm0_ref[...]

    ext_sc[CONV_TAIL:CONV_TAIL + rows, :] = xbc_ref[...]
    if rows < L:
        ext_sc[CONV_TAIL + rows:CONV_TAIL + L, :] = jnp.zeros((L - rows, SSD_CONV_DIM), F32)

    @pl.when(c == last)
    def _():
        nconv_ref[...] = ext_sc[CONV_TAIL + rows - (SSD_CONV - 1):CONV_TAIL + rows, :]

    cblk = 512
    for lo in range(0, SSD_CONV_DIM, cblk):
        acc = cb_ref[:, lo:lo + cblk] + ext_sc[CONV_TAIL:CONV_TAIL + L, lo:lo + cblk] * cw_ref[3:4, lo:lo + cblk]
        for k in range(1, SSD_CONV):
            w_row = cw_ref[3 - k:4 - k, lo:lo + cblk]
            acc = acc + ext_sc[CONV_TAIL - k:CONV_TAIL - k + L, lo:lo + cblk] * w_row
        xs_sc[:, lo:lo + cblk] = _silu(acc)

    ext_sc[0:CONV_TAIL, :] = ext_sc[L:L + CONV_TAIL, :]

    if rows < L:
        dt_raw = jnp.concatenate([dt_ref[...], jnp.zeros((L - rows, DT_PAD), F32)], axis=0)
    else:
        dt_raw = dt_ref[...]
    xx = dt_raw.T[0:SSD_HEADS, :] + dtb_ref[...]
    dt_t = jnp.maximum(xx, 0.0) + jnp.log1p(jnp.exp(-jnp.abs(xx)))
    if rows < L:
        col = lax.broadcasted_iota(jnp.int32, (SSD_HEADS, L), 1)
        dt_t = jnp.where(col < rows, dt_t, 0.0)
    adt_t = -jnp.exp(alog_ref[...]) * dt_t
    ri = lax.broadcasted_iota(jnp.int32, (L, L), 0)
    ci = lax.broadcasted_iota(jnp.int32, (L, L), 1)
    causal = ci <= ri
    triu = jnp.where(ri <= ci, 1.0, 0.0).astype(F32)
    acs_t = jnp.dot(adt_t, triu, preferred_element_type=F32, precision=lax.Precision.HIGHEST)
    acs = jnp.concatenate([acs_t, jnp.zeros((L - SSD_HEADS, L), F32)], axis=0).T
    acs_end_t = acs_t[:, L - 1:L]
    w_t = dt_t * jnp.exp(acs_end_t - acs_t)
    cdec_t = jnp.exp(acs_end_t)

    half = lax.broadcasted_iota(jnp.int32, (L, LANES), 1) < SSD_HEADDIM
    for g in range(SSD_GROUPS):
        bm = xs_sc[:, di + g * SSD_STATE:di + (g + 1) * SSD_STATE]
        cm = xs_sc[:, di + SSD_GN + g * SSD_STATE:di + SSD_GN + (g + 1) * SSD_STATE]
        bm_bf = bm.astype(BF16)
        cb = _nt_dot(cm.astype(BF16), bm_bf)
        for jj in range(SSD_HEADS // SSD_GROUPS // 2):
            j = g * (SSD_HEADS // SSD_GROUPS // 2) + jj
            ps = slice(j * LANES, (j + 1) * LANES)
            x_pair = xs_sc[:, ps]
            x_bf = x_pair.astype(BF16)
            st_pair = st_sc[ps, :]
            st_bf = st_pair.astype(BF16)
            ys = []
            for hh in range(2):
                hd = 2 * j + hh
                acs_col = jnp.broadcast_to(acs[:, hd:hd + 1], (L, L))
                seg = acs_col - acs_t[hd:hd + 1, :]
                dec = jnp.exp(jnp.where(causal, seg, -jnp.inf))
                mh = (cb * dec * dt_t[hd:hd + 1, :]).astype(BF16)
                y_h = jnp.dot(mh, x_bf, preferred_element_type=F32)
                cme = (cm * jnp.exp(acs_col)).astype(BF16)
                y_h = y_h + _nt_dot(cme, st_bf)
                ys.append(y_h)
            y_pair = jnp.where(half, ys[0], ys[1]) + x_pair * dsk_ref[:, ps]
            y_ref[:, ps] = y_pair[0:rows]
            x_t = x_pair.T
            w_rows = jnp.concatenate([jnp.broadcast_to(w_t[2 * j:2 * j + 1, :], (SSD_HEADDIM, L)),
                                      jnp.broadcast_to(w_t[2 * j + 1:2 * j + 2, :], (SSD_HEADDIM, L))], axis=0)
            d_rows = jnp.concatenate([jnp.broadcast_to(cdec_t[2 * j:2 * j + 1, :], (SSD_HEADDIM, 1)),
                                      jnp.broadcast_to(cdec_t[2 * j + 1:2 * j + 2, :], (SSD_HEADDIM, 1))], axis=0)
            upd = jnp.dot((x_t * w_rows).astype(BF16), bm_bf, preferred_element_type=F32)
            st_sc[ps, :] = st_pair * d_rows + upd

    @pl.when(c == last)
    def _():
        fin_ref[...] = st_sc[...]


def _ssd_scan_call(xbc, dt, conv0, ssm0, conv_w, conv_b, dt_bias, a_log, d_skip, layer, n_layers, fin_stack=None):
    bs, tt, _ = xbc.shape
    rows = SSD_CHUNK if tt % SSD_CHUNK == 0 else tt
    nc = tt // rows
    dtb = dt_bias.astype(F32).reshape(SSD_HEADS, 1)
    alog = a_log.astype(F32).reshape(SSD_HEADS, 1)
    dsk = jnp.repeat(d_skip.astype(F32), SSD_HEADDIM).reshape(1, SSD_D_INNER)

    def tok(width):
        return pl.BlockSpec((None, rows, width), lambda b, c: (b, c, 0))

    def per_seq(r, width):
        return pl.BlockSpec((None, r, width), lambda b, c: (b, 0, 0))

    in_specs = [tok(SSD_CONV_DIM), tok(DT_PAD), per_seq(SSD_CONV - 1, SSD_CONV_DIM),
                per_seq(SSD_D_INNER, SSD_STATE),
                _const_spec((SSD_CONV, SSD_CONV_DIM)), _const_spec((1, SSD_CONV_DIM)),
                _const_spec((SSD_HEADS, 1)), _const_spec((SSD_HEADS, 1)), _const_spec((1, SSD_D_INNER))]
    args = [xbc, dt, conv0, ssm0, conv_w, conv_b.reshape(1, SSD_CONV_DIM), dtb, alog, dsk]
    aliases = {}
    if fin_stack is not None:
        in_specs.append(pl.BlockSpec(memory_space=pl.ANY))
        aliases = {len(args): 1}
        args.append(fin_stack)
    fin_spec = pl.BlockSpec((None, None, SSD_D_INNER, SSD_STATE), lambda b, c: (layer, b, 0, 0))
    return pl.pallas_call(
        functools.partial(_ssd_scan_kernel, rows=rows),
        grid=(bs, nc),
        in_specs=in_specs,
        out_specs=[tok(SSD_D_INNER), fin_spec, per_seq(SSD_CONV - 1, SSD_CONV_DIM)],
        out_shape=[
            jax.ShapeDtypeStruct((bs, tt, SSD_D_INNER), F32),
            jax.ShapeDtypeStruct((n_layers, bs, SSD_D_INNER, SSD_STATE), F32),
            jax.ShapeDtypeStruct((bs, SSD_CONV - 1, SSD_CONV_DIM), F32),
        ],
        input_output_aliases=aliases,
        scratch_shapes=[
            pltpu.VMEM((SSD_CHUNK + CONV_TAIL, SSD_CONV_DIM), F32),
            pltpu.VMEM((SSD_CHUNK, SSD_CONV_DIM), F32),
            pltpu.VMEM((SSD_D_INNER, SSD_STATE), F32),
        ],
        compiler_params=_cparams("arbitrary", "arbitrary"),
        name="ssd_scan",
    )(*args)


def _ssd_out_kernel(y_ref, z_ref, ng_ref, w_ref, x_ref, gate_ref, g_ref, b_ref, o_ref, a_sc):
    gw = SSD_D_INNER // SSD_GROUPS
    for g in range(SSD_GROUPS):
        gs = slice(g * gw, (g + 1) * gw)
        yz = y_ref[:, gs] * _silu(z_ref[:, gs])
        ms = jnp.mean(yz * yz, axis=-1, keepdims=True)
        a_sc[:, gs] = (yz * lax.rsqrt(ms + RMS_EPS) * ng_ref[:, gs]).astype(BF16)
    out = jnp.dot(a_sc[...], w_ref[...], preferred_element_type=F32)
    r = DEEPNORM_ALPHA * x_ref[...] + gate_ref[...] * out
    o_ref[...] = _layer_norm(r, g_ref[...], b_ref[...])


def _ssd_out_call(y, z, norm_g, w_bf, x, mod, ln_g, ln_b):
    bx, tx, d = x.shape
    tm = min(512, tx)
    _, _, gate = _mod_specs(mod.shape[1], tm)

    def tok(width):
        return pl.BlockSpec((None, tm, width), lambda b, t: (b, t, 0))

    return pl.pallas_call(
        _ssd_out_kernel,
        grid=(bx, tx // tm),
        in_specs=[tok(SSD_D_INNER), tok(SSD_D_INNER), _const_spec((1, SSD_D_INNER)), _const_spec(w_bf.shape),
                  tok(d), gate, _const_spec((1, d)), _const_spec((1, d))],
        out_specs=tok(d),
        out_shape=jax.ShapeDtypeStruct((bx, tx, d), F32),
        scratch_shapes=[pltpu.VMEM((tm, SSD_D_INNER), BF16)],
        compiler_params=_cparams("arbitrary", "arbitrary"),
        name="ssd_out_proj",
    )(y, z, norm_g.reshape(1, SSD_D_INNER), w_bf, x, mod, ln_g.reshape(1, d), ln_b.reshape(1, d))


def _attention_layer(j, lam_init, y_p, y_s, mod_p, mod_s, cache_k, cache_v, page_table, tables, kv_p, kv_s,
                     attn_w_qkv, attn_lambda, attn_subln_g, attn_w_o, ln_g, ln_b):
    b, t, d = y_p.shape
    band, cfar, mask_far, mask_last, mask_new = tables
    w_qkv = attn_w_qkv[j].astype(BF16)
    w_o = attn_w_o[j].astype(BF16)
    lam_p = attn_lambda[j].astype(F32)
    g = attn_subln_g[j].astype(F32)

    n_layers = attn_w_qkv.shape[0]
    q_bf, k_p, v_p, k_bf, v_bf = _qkv_call(y_p, mod_p, w_qkv, j, n_layers, kv_p)
    o_p = _attn_prompt_call(q_bf, k_bf, v_bf, band, cfar, lam_p, g, lam_init)
    y_p = _oproj_call(o_p, w_o, y_p, mod_p, ln_g, ln_b)

    n_seq = page_table.shape[0]
    tq = y_s.shape[1] // n_seq
    n_pool = cache_k.shape[1]
    page_rows = PAGE_SIZE * ATT_HEADS
    pool_k = cache_k.reshape(cache_k.shape[0] * n_pool, page_rows, LANES)
    pool_v = cache_v.reshape(cache_v.shape[0] * n_pool, page_rows, LANES)
    q_bf, k_s, v_s, _, _ = _qkv_call(y_s, mod_s, w_qkv, j, n_layers, kv_s)
    new_rows = lambda a: a[j].reshape(n_seq, tq * ATT_HEADS, LANES)
    o_s = _attn_sample_call(q_bf.reshape(n_seq, tq, d), new_rows(k_s), new_rows(v_s), pool_k, pool_v,
                            page_table + j * n_pool, mask_far, mask_last, mask_new, lam_p, g, lam_init)
    y_s = _oproj_call(o_s.reshape(1, n_seq * tq, d), w_o, y_s, mod_s, ln_g, ln_b)

    return y_p, y_s, (k_p, v_p), (k_s, v_s)


def _ssd_layer(j, y_p, y_s, mod_p, mod_s, n_seq, fin_p, fin_s, state_ssm, state_conv, ssd_w_in, ssd_conv_w, ssd_conv_b,
               ssd_dt_bias, ssd_a_log, ssd_d, ssd_norm_g, ssd_w_out, ln_g, ln_b):
    b, t, d = y_p.shape
    tq = y_s.shape[1] // n_seq
    w_in = jnp.pad(ssd_w_in[j], ((0, 0), (0, DT_PAD - SSD_HEADS))).astype(BF16)
    w_out = ssd_w_out[j].astype(BF16)
    scan_w = (ssd_conv_w[j].astype(F32), ssd_conv_b[j].astype(F32), ssd_dt_bias[j], ssd_a_log[j], ssd_d[j])

    n_layers = ssd_w_in.shape[0]

    def mixer(y, mod, seqs, rows, conv0, ssm0, fin_stack):
        z, xbc, dt = _ssd_in_call(y, mod, w_in)
        per_seq = lambda a: a.reshape(seqs, rows, a.shape[-1])
        yy, fin, nconv = _ssd_scan_call(per_seq(xbc), per_seq(dt), conv0, ssm0, *scan_w, j, n_layers, fin_stack)
        yy = yy.reshape(y.shape[0], y.shape[1], SSD_D_INNER)
        y_new = _ssd_out_call(yy, z, ssd_norm_g[j].astype(F32), w_out, y, mod, ln_g, ln_b)
        return y_new, fin, nconv

    zero_conv = jnp.zeros((b, SSD_CONV - 1, SSD_CONV_DIM), F32)
    zero_ssm = jnp.zeros((b, SSD_D_INNER, SSD_STATE), F32)
    y_p, ssm_p, conv_p = mixer(y_p, mod_p, b, t, zero_conv, zero_ssm, fin_p)
    y_s, ssm_s, conv_s = mixer(y_s, mod_s, n_seq, tq, state_conv[j].astype(F32),
                               state_ssm[j].astype(F32).reshape(n_seq, SSD_D_INNER, SSD_STATE), fin_s)
    return y_p, y_s, ssm_p, conv_p, ssm_s, conv_s


def kernel(x_prompt, x_sample, cache_k, cache_v, state_ssm, state_conv, page_table, c_prompt, c_sample,
           ada_w, ada_b, ln_g, ln_b, rel_bias, attn_w_qkv, attn_lambda, attn_subln_g, attn_w_o,
           ssd_w_in, ssd_conv_w, ssd_conv_b, ssd_dt_bias, ssd_a_log, ssd_d, ssd_norm_g, ssd_w_out,
           ffn_w_in, ffn_w_out):
    b, t, d = x_prompt.shape
    n_seq, tq, _ = x_sample.shape
    depth = ada_w.shape[0]

    n_c = b + n_seq
    rc = -(-n_c // 8) * 8
    c_all = jnp.concatenate([c_prompt, c_sample, jnp.zeros((rc - n_c, d), F32)], axis=0)
    mod_all = _mod_call(c_all, ada_w.reshape(depth * 2, d, 3 * d), ada_b.reshape(depth * 2, 3 * d))

    def mods(i, s):
        m = mod_all[2 * i + s]
        mod_p = m[0:b].reshape(b, 1, 3 * d)
        mod_s = jnp.repeat(m[b:b + n_seq], tq, axis=0).reshape(1, n_seq * tq, 3 * d)
        return mod_p, mod_s

    past = page_table.shape[1] * PAGE_SIZE
    band = _diagonal_band(rel_bias, min(ATTN_NEAR_ROWS, ATTN_TILE, t))
    cfar = _bias_by_distance(rel_bias, jnp.asarray([REL_MAX_DIST], jnp.int32))[:, 0]
    tables = (band, cfar) + _sample_masks(rel_bias, tq, past)

    y_p = x_prompt
    y_s = x_sample.reshape(1, n_seq * tq, d)
    kv_p = kv_s = None
    fin_p = fin_s = None
    conv_p, conv_s = [], []
    for i in range(depth):
        j = i // 2
        mod_p, mod_s = mods(i, 0)
        if i % 2 == 0:
            lam_init = 0.8 - 0.6 * math.exp(-0.3 * i)
            y_p, y_s, kv_p, kv_s = _attention_layer(
                j, lam_init, y_p, y_s, mod_p, mod_s, cache_k, cache_v, page_table, tables, kv_p, kv_s,
                attn_w_qkv, attn_lambda, attn_subln_g, attn_w_o, ln_g[i, 0], ln_b[i, 0])
        else:
            y_p, y_s, fin_p, cp, fin_s, cs = _ssd_layer(
                j, y_p, y_s, mod_p, mod_s, n_seq, fin_p, fin_s, state_ssm, state_conv, ssd_w_in, ssd_conv_w, ssd_conv_b,
                ssd_dt_bias, ssd_a_log, ssd_d, ssd_norm_g, ssd_w_out, ln_g[i, 0], ln_b[i, 0])
            conv_p.append(cp)
            conv_s.append(cs)
        mod_p, mod_s = mods(i, 1)
        wi = ffn_w_in[i].astype(BF16)
        wo = ffn_w_out[i].astype(BF16)
        y_p = _ffn_call(y_p, mod_p, wi, wo, ln_g[i, 1], ln_b[i, 1])
        y_s = _ffn_call(y_s, mod_s, wi, wo, ln_g[i, 1], ln_b[i, 1])
    n_attn = attn_w_qkv.shape[0]
    kv_out = [a.reshape(n_attn, n, tt, ATT_HEADS, 2 * ATT_HD) for a, n, tt in
              ((kv_p[0], b, t), (kv_p[1], b, t), (kv_s[0], n_seq, tq), (kv_s[1], n_seq, tq))]
    return (y_p, y_s.reshape(n_seq, tq, d), *kv_out,
            fin_p.reshape(fin_p.shape[0], b, SSD_HEADS, SSD_HEADDIM, SSD_STATE), jnp.stack(conv_p),
            fin_s.reshape(fin_s.shape[0], n_seq, SSD_HEADS, SSD_HEADDIM, SSD_STATE), jnp.stack(conv_s))
```
